```python
import jax
import jax.numpy as jnp
from jax import lax
import numpy as np

D_MODEL = 1024
BATCH = 8
SEQ = 4096
DEPTH = 2

CHUNK = 64
Q_BLOCK = 128
A_WIDTH = D_MODEL // 2
A_GROUPS = 8
A_GROUP_DIM = A_WIDTH // A_GROUPS
GMLP_CHUNK = 128
HEAD_DIM = 64
B_HEADS = 8
B_WIDTH = B_HEADS * HEAD_DIM
EVEN_IN = 2 * A_WIDTH + 3 * B_WIDTH
C_HEADS = D_MODEL // HEAD_DIM
C_KV_GROUPS = 4
C_HEADS_PER_GROUP = C_HEADS // C_KV_GROUPS
C_WIDTH = C_HEADS * HEAD_DIM
C_KV_WIDTH = C_KV_GROUPS * HEAD_DIM
IDX_HEADS = 8
IDX_DIM = 64
IDX_TOPK_MAX = 256
ODD_SPLITS = [int(s) for s in np.cumsum([C_WIDTH, C_KV_WIDTH, C_KV_WIDTH, IDX_HEADS * IDX_DIM, IDX_DIM])]
ODD_IN = ODD_SPLITS[-1] + IDX_HEADS
N_EXPERTS = 32
TOP_K = 4
D_EXPERT = D_MODEL
SWIGLU_ALPHA = 1.702
SWIGLU_LIMIT = 7.0
MOE_BLOCK = 256
PLE_DIM = 256
LN_EPS = 1e-5
N_EVEN = (DEPTH + 1) // 2
N_ODD = DEPTH // 2
DEEPNORM_ALPHA = (2 * DEPTH) ** 0.25
DEEPNORM_BETA = (8 * DEPTH) ** -0.25

kernel_name = 'hybrid_gmlp_stickbreak_dsa_moe'


def layer_norm(x, g, b):
    xf = x.astype(jnp.float32)
    mu = jnp.mean(xf, -1, keepdims=True)
    var = jnp.mean(jnp.square(xf - mu), -1, keepdims=True)
    return ((xf - mu) * lax.rsqrt(var + LN_EPS) * g + b).astype(x.dtype)


def to_blocks(t):
    b, s = t.shape[:2]
    return jnp.swapaxes(t.reshape(b, s // Q_BLOCK, Q_BLOCK, *t.shape[2:]), 0, 1)


def from_blocks(t):
    nb, b, qb = t.shape[:3]
    return jnp.swapaxes(t, 0, 1).reshape(b, nb * qb, *t.shape[3:])


def spatial_gating(u, v, ln_g, ln_b, w_s, b_s):
    b, s, _ = u.shape
    v = layer_norm(v, ln_g, ln_b)
    vg = v.reshape(b, s // GMLP_CHUNK, GMLP_CHUNK, A_GROUPS, A_GROUP_DIM)
    w = w_s * jnp.tril(jnp.ones((GMLP_CHUNK, GMLP_CHUNK), w_s.dtype))
    sv = jnp.einsum('gts,bnsgc->bntgc', w, vg) + b_s.T[:, :, None]
    return u * sv.reshape(b, s, A_WIDTH)


def stick_breaking_attention(q, k, v):
    b, s, h, dh = q.shape
    scale = dh ** -0.5
    pos = jnp.arange(s, dtype=jnp.int32)
    starts = jnp.arange(s // Q_BLOCK, dtype=jnp.int32) * Q_BLOCK

    def block(args):
        qb, t0 = args
        z = jnp.einsum('bthd,bshd->bhts', qb, k).astype(jnp.float32) * scale
        t = t0 + jnp.arange(Q_BLOCK, dtype=jnp.int32)
        strict = pos[None, :] < t[:, None]
        log_not = jnp.where(strict, jax.nn.log_sigmoid(-z), 0.0)
        after = lax.cumsum(log_not, axis=3, reverse=True) - log_not
        w = jnp.where(strict, jnp.exp(jax.nn.log_sigmoid(z) + after), 0.0)
        return jnp.einsum('bhts,bshd->bthd', w.astype(v.dtype), v)

    out = lax.map(block, (to_blocks(q), starts))
    return from_blocks(out).reshape(b, s, h * dh)


def even_mixer(x, w_in, ln_g, ln_b, w_s, b_s, w_o):
    b, s, _ = x.shape
    z = x @ w_in
    a_part = jax.nn.gelu(z[..., :2 * A_WIDTH])
    u, v = a_part[..., :A_WIDTH], a_part[..., A_WIDTH:]
    q, k, vv = jnp.split(z[..., 2 * A_WIDTH:], 3, axis=-1)
    heads = lambda t: t.reshape(b, s, B_HEADS, HEAD_DIM)
    a_out = spatial_gating(u, v, ln_g, ln_b, w_s, b_s)
    b_out = stick_breaking_attention(heads(q), heads(k), heads(vv))
    return jnp.concatenate([a_out, b_out], axis=-1) @ w_o


def odd_mixer(x, w_in, w_o):
    b, s, _ = x.shape
    z = x @ w_in
    q, k, v, q_idx, k_idx, w_idx = jnp.split(z, ODD_SPLITS, axis=-1)
    q = q.reshape(b, s, C_KV_GROUPS, C_HEADS_PER_GROUP, HEAD_DIM)
    k = k.reshape(b, s, C_KV_GROUPS, HEAD_DIM)
    v = v.reshape(b, s, C_KV_GROUPS, HEAD_DIM)
    q_idx = q_idx.reshape(b, s, IDX_HEADS, IDX_DIM)
    w_idx = w_idx * (IDX_HEADS ** -0.5)
    topk = min(IDX_TOPK_MAX, s // 4)
    slopes = (2.0 ** (-8.0 * jnp.arange(1, C_HEADS + 1, dtype=jnp.float32) / C_HEADS)).reshape(C_KV_GROUPS, C_HEADS_PER_GROUP)
    pos = jnp.arange(s, dtype=jnp.int32)
    starts = jnp.arange(s // Q_BLOCK, dtype=jnp.int32) * Q_BLOCK
    gather = jax.vmap(lambda kk, ii: kk[ii])

    def block(args):
        qb, qib, wb, t0 = args
        t = t0 + jnp.arange(Q_BLOCK, dtype=jnp.int32)
        visible_end = (t // CHUNK + 1) * CHUNK
        admissible = pos[None, :] < visible_end[:, None]
        rel = jax.nn.relu(jnp.einsum('bthc,bsc->bths', qib, k_idx).astype(jnp.float32) * (IDX_DIM ** -0.5))
        iscore = jnp.einsum('bth,bths->bts', wb.astype(jnp.float32), rel)
        iscore = jnp.where(admissible[None], iscore, -jnp.inf)
        _, idx = lax.top_k(iscore, topk)
        valid = idx < visible_end[None, :, None]
        k_sel = gather(k, idx)
        v_sel = gather(v, idx)
        logits = jnp.einsum('btgrd,btkgd->btgrk', qb, k_sel).astype(jnp.float32) * (HEAD_DIM ** -0.5)
        dist = jnp.abs(t[None, :, None] - idx).astype(jnp.float32)
        logits = logits - slopes[None, None, :, :, None] * dist[:, :, None, None, :]
        logits = jnp.where(valid[:, :, None, None, :], logits, -jnp.inf)
        probs = jax.nn.softmax(logits, axis=-1)
        o = jnp.einsum('btgrk,btkgd->btgrd', probs.astype(v.dtype), v_sel)
        return o.reshape(b, Q_BLOCK, C_WIDTH)

    out = lax.map(block, (to_blocks(q), to_blocks(q_idx), to_blocks(w_idx), starts))
    return from_blocks(out) @ w_o


def moe(h, w_router, b_router, w_gu, b_gu, w_down, b_down):
    b, s, d = h.shape
    n_tok = b * s
    n_asg = n_tok * TOP_K
    xt = h.reshape(n_tok, d)
    logits = (xt @ w_router + b_router).astype(jnp.float32)
    top_vals, top_idx = lax.top_k(logits, TOP_K)
    gates = jax.nn.softmax(top_vals, axis=-1)
    flat_e = top_idx.reshape(-1)
    flat_tok = jnp.arange(n_asg, dtype=jnp.int32) // TOP_K
    flat_gate = gates.reshape(-1)
    order = jnp.argsort(flat_e)
    sorted_e = flat_e[order]
    counts = jnp.bincount(flat_e, length=N_EXPERTS)
    padded = (counts + MOE_BLOCK - 1) // MOE_BLOCK * MOE_BLOCK
    grp_starts = jnp.cumsum(counts) - counts
    pad_ends = jnp.cumsum(padded)
    pad_starts = pad_ends - padded
    dest = pad_starts[sorted_e] + jnp.arange(n_asg, dtype=jnp.int32) - grp_starts[sorted_e]
    n_blocks = n_asg // MOE_BLOCK + N_EXPERTS
    n_rows = n_blocks * MOE_BLOCK
    row_tok = jnp.full((n_rows,), n_tok, jnp.int32).at[dest].set(flat_tok[order])
    row_gate = jnp.zeros((n_rows,), jnp.float32).at[dest].set(flat_gate[order])
    block_e = jnp.minimum(jnp.searchsorted(pad_ends, jnp.arange(n_blocks, dtype=pad_ends.dtype) * MOE_BLOCK, side='right'), N_EXPERTS - 1)
    x_pad = jnp.concatenate([xt, jnp.zeros((1, d), xt.dtype)], axis=0)

    def expert_block(args):
        tok, g, e = args
        xb = x_pad[tok]
        gu = xb @ w_gu[e] + b_gu[e]
        gate = jnp.minimum(gu[:, :D_EXPERT], SWIGLU_LIMIT)
        up = jnp.clip(gu[:, D_EXPERT:], -SWIGLU_LIMIT, SWIGLU_LIMIT)
        act = gate * jax.nn.sigmoid(SWIGLU_ALPHA * gate) * (up + 1.0)
        y = act @ w_down[e] + b_down[e]
        return y * g[:, None].astype(y.dtype)

    ys = lax.map(expert_block, (row_tok.reshape(n_blocks, MOE_BLOCK), row_gate.reshape(n_blocks, MOE_BLOCK), block_e))
    out = jax.ops.segment_sum(ys.reshape(n_rows, d), row_tok, num_segments=n_tok + 1)[:n_tok]
    return out.reshape(b, s, d)


def _normal(key, shape, scale):
    return jax.random.normal(key, shape, jnp.float32) * scale


def setup_inputs(seed: int = 0) -> dict:
    key = jax.random.key(seed)
    ks = jax.random.split(key, 24)
    D = D_MODEL
    return {
        'x': _normal(ks[0], (BATCH, SEQ, D), 1.0),
        'p': _normal(ks[1], (DEPTH, BATCH, SEQ, PLE_DIM), 1.0),
        'even_w_in': _normal(ks[2], (N_EVEN, D, EVEN_IN), D ** -0.5),
        'even_gmlp_ln_g': 1.0 + _normal(ks[3], (N_EVEN, A_WIDTH), 0.02),
        'even_gmlp_ln_b': _normal(ks[4], (N_EVEN, A_WIDTH), 0.02),
        'even_w_s': _normal(ks[5], (N_EVEN, A_GROUPS, GMLP_CHUNK, GMLP_CHUNK), GMLP_CHUNK ** -0.5),
        'even_b_s': 1.0 + _normal(ks[6], (N_EVEN, A_GROUPS, GMLP_CHUNK), 0.1),
        'even_w_o': _normal(ks[7], (N_EVEN, D, D), DEEPNORM_BETA * D ** -0.5),
        'odd_w_in': _normal(ks[8], (N_ODD, D, ODD_IN), D ** -0.5),
        'odd_w_o': _normal(ks[9], (N_ODD, C_WIDTH, D), DEEPNORM_BETA * C_WIDTH ** -0.5),
        'ln1_g': 1.0 + _normal(ks[10], (DEPTH, D), 0.02),
        'ln1_b': _normal(ks[11], (DEPTH, D), 0.02),
        'w_router': _normal(ks[12], (DEPTH, D, N_EXPERTS), D ** -0.5),
        'b_router': _normal(ks[13], (DEPTH, N_EXPERTS), 0.01),
        'w_gu': _normal(ks[14], (DEPTH, N_EXPERTS, D, 2 * D_EXPERT), D ** -0.5),
        'b_gu': _normal(ks[15], (DEPTH, N_EXPERTS, 2 * D_EXPERT), 0.01),
        'w_down': _normal(ks[16], (DEPTH, N_EXPERTS, D_EXPERT, D), DEEPNORM_BETA * D_EXPERT ** -0.5),
        'b_down': _normal(ks[17], (DEPTH, N_EXPERTS, D), 0.01),
        'ln2_g': 1.0 + _normal(ks[18], (DEPTH, D), 0.02),
        'ln2_b': _normal(ks[19], (DEPTH, D), 0.02),
        'w_ple': _normal(ks[20], (DEPTH, PLE_DIM, D), PLE_DIM ** -0.5),
        'w_ple_gate': _normal(ks[21], (DEPTH, D, D), D ** -0.5),
        'b_ple_gate': _normal(ks[22], (DEPTH, D), 0.01),
    }


def reference(x, p, even_w_in, even_gmlp_ln_g, even_gmlp_ln_b, even_w_s, even_b_s, even_w_o, odd_w_in, odd_w_o, ln1_g, ln1_b, w_router, b_router, w_gu, b_gu, w_down, b_down, ln2_g, ln2_b, w_ple, w_ple_gate, b_ple_gate):
    for i in range(DEPTH):
        j = i // 2
        if i % 2 == 0:
            mix = even_mixer(x, even_w_in[j], even_gmlp_ln_g[j], even_gmlp_ln_b[j], even_w_s[j], even_b_s[j], even_w_o[j])
        else:
            mix = odd_mixer(x, odd_w_in[j], odd_w_o[j])
        h = layer_norm(DEEPNORM_ALPHA * x + mix, ln1_g[i], ln1_b[i])
        mid = DEEPNORM_ALPHA * h + moe(h, w_router[i], b_router[i], w_gu[i], b_gu[i], w_down[i], b_down[i])
        gate = jax.nn.sigmoid(mid @ w_ple_gate[i] + b_ple_gate[i])
        x = layer_norm(mid + gate * (p[i] @ w_ple[i]), ln2_g[i], ln2_b[i])
    return x
```

```python
import functools

import jax
import jax.numpy as jnp
from jax import lax
from jax.experimental import pallas as pl
from jax.experimental.pallas import tpu as pltpu

F32 = jnp.float32
BF16 = jnp.bfloat16
I32 = jnp.int32

D_MODEL = 1024
DEPTH = 2
CHUNK = 64
A_WIDTH = 512
A_GROUPS = 8
GMLP_CHUNK = 128
HEAD_DIM = 64
B_HEADS = 8
B_WIDTH = 512
C_HEADS = 16
C_KV_GROUPS = 4
C_HEADS_PER_GROUP = 4
C_WIDTH = 1024
C_KV_WIDTH = 256
IDX_HEADS = 8
IDX_DIM = 64
IDX_TOPK_MAX = 256
N_EXPERTS = 32
TOP_K = 4
D_EXPERT = 1024
SWIGLU_ALPHA = 1.702
SWIGLU_LIMIT = 7.0
MOE_BLOCK = 256
PLE_DIM = 256
LN_EPS = 1e-5
DEEPNORM_ALPHA = (2 * DEPTH) ** 0.25

LANES = 128
VMEM_LIMIT = 48 * 1024 * 1024

NEG_BIG = -1e30
INT_MIN = -(2 ** 31)


def _cparams(*sem):
    return pltpu.CompilerParams(dimension_semantics=sem, vmem_limit_bytes=VMEM_LIMIT)


def _dot(a, b):
    return jnp.dot(a, b, preferred_element_type=F32)


def _dot_nt(a, b):
    return lax.dot_general(a, b, (((1,), (1,)), ((), ())), preferred_element_type=F32)


def _layer_norm(y, g, b):
    mu = jnp.mean(y, axis=-1, keepdims=True)
    yc = y - mu
    var = jnp.mean(yc * yc, axis=-1, keepdims=True)
    return yc * lax.rsqrt(var + LN_EPS) * g + b


def _sigmoid(x):
    return 1.0 / (1.0 + jnp.exp(-x))


EVEN_TM = 256


def _even_in_kernel(x_ref, w_ref, lng_ref, lnb_ref, ws_ref, bs_ref, a_ref, qkv_ref):
    tm = x_ref.shape[0]
    xb = x_ref[...].astype(BF16)
    za = _dot(xb, w_ref[:, : 2 * A_WIDTH])
    ga = 0.5 * za * (1.0 + jnp.tanh(0.7978845608028654 * (za + 0.044715 * (za * za * za))))
    u = ga[:, :A_WIDTH]
    vn = _layer_norm(ga[:, A_WIDTH:], lng_ref[...], lnb_ref[...]).astype(BF16)

    r = lax.broadcasted_iota(I32, (GMLP_CHUNK, GMLP_CHUNK), 0)
    c = lax.broadcasted_iota(I32, (GMLP_CHUNK, GMLP_CHUNK), 1)
    tril = c <= r
    w_s = [jnp.where(tril, ws_ref[g], 0.0).astype(BF16) for g in range(A_GROUPS)]
    low_half = lax.broadcasted_iota(I32, (GMLP_CHUNK, LANES), 1) < A_WIDTH // A_GROUPS
    for ci in range(tm // GMLP_CHUNK):
        rows = slice(ci * GMLP_CHUNK, (ci + 1) * GMLP_CHUNK)
        for p in range(A_GROUPS // 2):
            cols = slice(p * LANES, (p + 1) * LANES)
            vb = vn[rows, cols]
            sv = jnp.where(low_half, _dot(w_s[2 * p], vb), _dot(w_s[2 * p + 1], vb)) + bs_ref[:, cols]
            a_ref[rows, cols] = (u[rows, cols] * sv).astype(BF16)

    nq = qkv_ref.shape[1]
    for j in range(0, nq, 512):
        qkv_ref[:, j:j + 512] = _dot(xb, w_ref[:, 2 * A_WIDTH + j: 2 * A_WIDTH + j + 512]).astype(BF16)


def _even_in(x2, w_in, ln_g, ln_b, w_s, bs_full):
    n, d = x2.shape
    nq = w_in.shape[1] - 2 * A_WIDTH
    return pl.pallas_call(
        _even_in_kernel,
        grid=(n // EVEN_TM,),
        in_specs=[
            pl.BlockSpec((EVEN_TM, d), lambda i: (i, 0)),
            pl.BlockSpec(w_in.shape, lambda i: (0, 0)),
            pl.BlockSpec((1, A_WIDTH), lambda i: (0, 0)),
            pl.BlockSpec((1, A_WIDTH), lambda i: (0, 0)),
            pl.BlockSpec(w_s.shape, lambda i: (0, 0, 0)),
            pl.BlockSpec(bs_full.shape, lambda i: (0, 0)),
        ],
        out_specs=[
            pl.BlockSpec((EVEN_TM, A_WIDTH), lambda i: (i, 0)),
            pl.BlockSpec((EVEN_TM, nq), lambda i: (i, 0)),
        ],
        out_shape=[
            jax.ShapeDtypeStruct((n, A_WIDTH), BF16),
            jax.ShapeDtypeStruct((n, nq), BF16),
        ],
        compiler_params=_cparams("parallel"),
        name="even_in",
    )(x2, w_in, ln_g, ln_b, w_s, bs_full)


SB_T = 256


def _sb_kernel(q_ref, k_ref, v_ref, o_ref):
    t = q_ref.shape[0]
    qi = pl.program_id(2)
    q = q_ref[...]
    lane = lax.broadcasted_iota(I32, (t, LANES), 1)
    r = lax.broadcasted_iota(I32, (t, t), 0)
    c = lax.broadcasted_iota(I32, (t, t), 1)
    suffix = (r > c).astype(BF16)
    strict = c < r

    def block(qh, j, carry, acc, diag):
        start = pl.multiple_of(j * t, t)
        kb = k_ref[pl.ds(start, t), :]
        vb = v_ref[pl.ds(start, t), :]
        s = _dot_nt(qh, kb)
        log_not = -(jnp.maximum(s, 0.0) + jnp.log(1.0 + jnp.exp(-jnp.abs(s))))
        if diag:
            log_not = jnp.where(strict, log_not, 0.0)
        hi = log_not.astype(BF16)
        lo = (log_not - hi.astype(F32)).astype(BF16)
        after = _dot(hi, suffix) + _dot(lo, suffix) + carry
        w = jnp.exp(s + log_not + after)
        if diag:
            w = jnp.where(strict, w, 0.0)
        acc = acc + _dot(w.astype(BF16), vb)
        carry = carry + jnp.sum(log_not, axis=1, keepdims=True)
        return carry, acc

    outs = []
    for hh in range(2):
        sel = (lane < HEAD_DIM) if hh == 0 else (lane >= HEAD_DIM)
        qh = jnp.where(sel, q, jnp.zeros_like(q)) * (HEAD_DIM ** -0.5)
        carry, acc = block(qh, qi, jnp.zeros((t, 1), F32), jnp.zeros((t, LANES), F32), True)

        def body(it, ca, qh=qh):
            return block(qh, qi - 1 - it, ca[0], ca[1], False)

        carry, acc = lax.fori_loop(0, qi, body, (carry, acc))
        outs.append(acc)
    o_ref[...] = jnp.where(lane < HEAD_DIM, outs[0], outs[1]).astype(BF16)


def _stick_breaking(qkv3):
    b, s, _ = qkv3.shape
    npair = B_WIDTH // LANES
    return pl.pallas_call(
        _sb_kernel,
        grid=(b, npair, s // SB_T),
        in_specs=[
            pl.BlockSpec((None, SB_T, LANES), lambda bi, p, qi: (bi, qi, p)),
            pl.BlockSpec((None, s, LANES), lambda bi, p, qi: (bi, 0, npair + p)),
            pl.BlockSpec((None, s, LANES), lambda bi, p, qi: (bi, 0, 2 * npair + p)),
        ],
        out_specs=pl.BlockSpec((None, SB_T, LANES), lambda bi, p, qi: (bi, qi, p)),
        out_shape=jax.ShapeDtypeStruct((b, s, B_WIDTH), BF16),
        compiler_params=_cparams("parallel", "parallel", "arbitrary"),
        name="stick_breaking",
    )(qkv3, qkv3, qkv3)


POST_TM = 256


def _post_mixer_kernel(*refs, n_in):
    ins = refs[:n_in]
    ws = refs[n_in:2 * n_in]
    x_ref, g_ref, b_ref, wrh_ref, wrl_ref, br_ref, h_ref, idx_ref, gate_ref = refs[2 * n_in:]
    tm = x_ref.shape[0]
    mix = _dot(ins[0][...], ws[0][...])
    for a, w in zip(ins[1:], ws[1:]):
        mix = mix + _dot(a[...], w[...])
    h = _layer_norm(DEEPNORM_ALPHA * x_ref[...] + mix, g_ref[...], b_ref[...])
    h_ref[...] = h

    hh = h.astype(BF16)
    hl = (h - hh.astype(F32)).astype(BF16)
    logits = _dot(hh, wrh_ref[...]) + _dot(hl, wrh_ref[...]) + _dot(hh, wrl_ref[...]) + br_ref[...]

    lane = lax.broadcasted_iota(I32, (tm, LANES), 1)
    vals, idxs = [], []
    for _ in range(TOP_K):
        m = jnp.max(logits, axis=1, keepdims=True)
        i = jnp.min(jnp.where(logits == m, lane, LANES), axis=1, keepdims=True)
        vals.append(m)
        idxs.append(i)
        logits = jnp.where(lane == i, NEG_BIG * 2.0, logits)
    es = [jnp.exp(v - vals[0]) for v in vals]
    inv = 1.0 / (es[0] + es[1] + es[2] + es[3])
    idx_out = jnp.zeros((tm, LANES), I32)
    gate_out = jnp.zeros((tm, LANES), F32)
    for k in range(TOP_K):
        idx_out = jnp.where(lane == k, idxs[k], idx_out)
        gate_out = jnp.where(lane == k, es[k] * inv, gate_out)
    idx_ref[...] = idx_out
    gate_ref[...] = gate_out


def _post_mixer(parts, weights, x2, ln_g, ln_b, wr_hi, wr_lo, br):
    n, d = x2.shape
    n_in = len(parts)
    in_specs = [pl.BlockSpec((POST_TM, a.shape[1]), lambda i: (i, 0)) for a in parts]
    in_specs += [pl.BlockSpec(w.shape, lambda i: (0, 0)) for w in weights]
    in_specs += [
        pl.BlockSpec((POST_TM, d), lambda i: (i, 0)),
        pl.BlockSpec((1, d), lambda i: (0, 0)),
        pl.BlockSpec((1, d), lambda i: (0, 0)),
        pl.BlockSpec(wr_hi.shape, lambda i: (0, 0)),
        pl.BlockSpec(wr_lo.shape, lambda i: (0, 0)),
        pl.BlockSpec((1, LANES), lambda i: (0, 0)),
    ]
    return pl.pallas_call(
        functools.partial(_post_mixer_kernel, n_in=n_in),
        grid=(n // POST_TM,),
        in_specs=in_specs,
        out_specs=[
            pl.BlockSpec((POST_TM, d), lambda i: (i, 0)),
            pl.BlockSpec((POST_TM, LANES), lambda i: (i, 0)),
            pl.BlockSpec((POST_TM, LANES), lambda i: (i, 0)),
        ],
        out_shape=[
            jax.ShapeDtypeStruct((n, d), F32),
            jax.ShapeDtypeStruct((n, LANES), I32),
            jax.ShapeDtypeStruct((n, LANES), F32),
        ],
        compiler_params=_cparams("parallel"),
        name="post_mixer",
    )(*parts, *weights, x2, ln_g, ln_b, wr_hi, wr_lo, br)


def _row_copy(src_hbm, src_row, dst_ref, dst_row, sem):
    return pltpu.make_async_copy(src_hbm.at[pl.ds(src_row, 1), :], dst_ref.at[pl.ds(dst_row, 1), :], sem)


def _gather_rows_kernel(tok_ref, h_hbm, o_ref, sem):
    rows = o_ref.shape[0]

    def start(r, c):
        _row_copy(h_hbm, tok_ref[0, 0, r], o_ref, r, sem).start()
        return c

    lax.fori_loop(0, rows, start, 0, unroll=8)

    def wait(r, c):
        _row_copy(h_hbm, 0, o_ref, r, sem).wait()
        return c

    lax.fori_loop(0, rows, wait, 0, unroll=8)


def _gather_rows(h, row_tok):
    n_rows = row_tok.shape[0]
    d = h.shape[1]
    nb = n_rows // MOE_BLOCK
    return pl.pallas_call(
        _gather_rows_kernel,
        grid=(nb,),
        in_specs=[
            pl.BlockSpec((1, 1, MOE_BLOCK), lambda i: (i, 0, 0), memory_space=pltpu.SMEM),
            pl.BlockSpec(memory_space=pl.ANY),
        ],
        out_specs=pl.BlockSpec((MOE_BLOCK, d), lambda i: (i, 0)),
        out_shape=jax.ShapeDtypeStruct((n_rows, d), h.dtype),
        scratch_shapes=[pltpu.SemaphoreType.DMA],
        compiler_params=_cparams("arbitrary"),
        name="moe_gather",
    )(row_tok.reshape(nb, 1, MOE_BLOCK), h)


def _expert_kernel(be_ref, x_ref, wgu_ref, bgu_ref, wd_ref, bd_ref, g_ref, y_ref):
    del be_ref
    xb = x_ref[...].astype(BF16)
    gu = _dot(xb, wgu_ref[...]) + bgu_ref[...]
    gate = jnp.minimum(gu[:, :D_EXPERT], SWIGLU_LIMIT)
    up = jnp.clip(gu[:, D_EXPERT:], -SWIGLU_LIMIT, SWIGLU_LIMIT)
    act = gate * _sigmoid(SWIGLU_ALPHA * gate) * (up + 1.0)
    y = _dot(act.astype(BF16), wd_ref[...]) + bd_ref[...]
    y_ref[...] = y * g_ref[...]


def _experts(xs, block_e, w_gu, b_gu, w_down, b_down, row_gate):
    n_rows, d = xs.shape
    nb = n_rows // MOE_BLOCK
    return pl.pallas_call(
        _expert_kernel,
        grid_spec=pltpu.PrefetchScalarGridSpec(
            num_scalar_prefetch=1,
            grid=(nb,),
            in_specs=[
                pl.BlockSpec((MOE_BLOCK, d), lambda i, be: (i, 0)),
                pl.BlockSpec((None, d, 2 * D_EXPERT), lambda i, be: (be[i], 0, 0)),
                pl.BlockSpec((None, 1, 2 * D_EXPERT), lambda i, be: (be[i], 0, 0)),
                pl.BlockSpec((None, D_EXPERT, d), lambda i, be: (be[i], 0, 0)),
                pl.BlockSpec((None, 1, d), lambda i, be: (be[i], 0, 0)),
                pl.BlockSpec((MOE_BLOCK, 1), lambda i, be: (i, 0)),
            ],
            out_specs=pl.BlockSpec((MOE_BLOCK, d), lambda i, be: (i, 0)),
        ),
        out_shape=jax.ShapeDtypeStruct((n_rows, d), F32),
        compiler_params=_cparams("arbitrary"),
        name="moe_experts",
    )(block_e, xs, w_gu, b_gu, w_down, b_down, row_gate)


COMB_TM = 256


def _combine_kernel(dest_ref, ys_hbm, h_ref, p_ref, wpg_ref, bpg_ref, wple_ref, g_ref, b_ref, o_ref, buf, sem):
    tm = h_ref.shape[0]

    def start(r, c):
        for k in range(TOP_K):
            _row_copy(ys_hbm, dest_ref[0, 0, r * TOP_K + k], buf.at[k], r, sem).start()
        return c

    lax.fori_loop(0, tm, start, 0, unroll=4)

    def wait(r, c):
        for k in range(TOP_K):
            _row_copy(ys_hbm, 0, buf.at[k], r, sem).wait()
        return c

    lax.fori_loop(0, tm, wait, 0, unroll=4)

    moe = (buf[0] + buf[1]) + (buf[2] + buf[3])
    mid = DEEPNORM_ALPHA * h_ref[...] + moe
    gate = _sigmoid(_dot(mid.astype(BF16), wpg_ref[...]) + bpg_ref[...])
    ple = _dot(p_ref[...].astype(BF16), wple_ref[...])
    o_ref[...] = _layer_norm(mid + gate * ple, g_ref[...], b_ref[...])


def _combine(dest, ys, h, p2, w_pg, b_pg, w_ple, ln_g, ln_b):
    n, d = h.shape
    nt = n // COMB_TM
    return pl.pallas_call(
        _combine_kernel,
        grid=(nt,),
        in_specs=[
            pl.BlockSpec((1, 1, COMB_TM * TOP_K), lambda i: (i, 0, 0), memory_space=pltpu.SMEM),
            pl.BlockSpec(memory_space=pl.ANY),
            pl.BlockSpec((COMB_TM, d), lambda i: (i, 0)),
            pl.BlockSpec((COMB_TM, PLE_DIM), lambda i: (i, 0)),
            pl.BlockSpec(w_pg.shape, lambda i: (0, 0)),
            pl.BlockSpec((1, d), lambda i: (0, 0)),
            pl.BlockSpec(w_ple.shape, lambda i: (0, 0)),
            pl.BlockSpec((1, d), lambda i: (0, 0)),
            pl.BlockSpec((1, d), lambda i: (0, 0)),
        ],
        out_specs=pl.BlockSpec((COMB_TM, d), lambda i: (i, 0)),
        out_shape=jax.ShapeDtypeStruct((n, d), F32),
        scratch_shapes=[pltpu.VMEM((TOP_K, COMB_TM, d), F32), pltpu.SemaphoreType.DMA],
        compiler_params=_cparams("arbitrary"),
        name="moe_combine",
    )(dest.reshape(nt, 1, COMB_TM * TOP_K), ys, h, p2, w_pg, b_pg, w_ple, ln_g, ln_b)


def _moe_routing(top_idx, gates, n_tok):
    n_asg = n_tok * TOP_K
    n_blocks = n_asg // MOE_BLOCK + N_EXPERTS
    n_rows = n_blocks * MOE_BLOCK
    sel = (top_idx[:, :, None] == jnp.arange(N_EXPERTS, dtype=I32)[None, None, :]).any(axis=1).astype(I32)
    pos = jnp.cumsum(sel, axis=0) - sel
    counts = jnp.sum(sel, axis=0)
    padded = (counts + MOE_BLOCK - 1) // MOE_BLOCK * MOE_BLOCK
    pad_ends = jnp.cumsum(padded)
    pad_starts = pad_ends - padded
    dest = pad_starts[top_idx] + jnp.take_along_axis(pos, top_idx, axis=1)
    flat_dest = dest.reshape(-1)
    flat_tok = jnp.arange(n_asg, dtype=I32) // TOP_K
    row_tok = jnp.zeros((n_rows,), I32).at[flat_dest].set(flat_tok)
    row_gate = jnp.zeros((n_rows,), F32).at[flat_dest].set(gates.reshape(-1))
    block_e = jnp.minimum(
        jnp.searchsorted(pad_ends, jnp.arange(n_blocks, dtype=pad_ends.dtype) * MOE_BLOCK, side="right"),
        N_EXPERTS - 1).astype(I32)
    return dest.astype(I32), row_tok, row_gate, block_e


ODD_TM = 512
ODD_TAIL = 128


def _odd_in_kernel(x_ref, w_ref, q_ref, kv_ref, qi_ref, tail_ref):
    xb = x_ref[...].astype(BF16)
    off = 0
    for ref in (q_ref, kv_ref, qi_ref):
        n = ref.shape[1]
        for j in range(0, n, 512):
            ref[:, j:j + 512] = _dot(xb, w_ref[:, off + j: off + j + 512]).astype(ref.dtype)
        off += n
    tail_ref[...] = _dot(xb, w_ref[:, off:off + ODD_TAIL])


def _odd_in(x2, w_in):
    n, d = x2.shape
    widths = (C_WIDTH, 2 * C_KV_WIDTH, IDX_HEADS * IDX_DIM)
    return pl.pallas_call(
        _odd_in_kernel,
        grid=(n // ODD_TM,),
        in_specs=[
            pl.BlockSpec((ODD_TM, d), lambda i: (i, 0)),
            pl.BlockSpec(w_in.shape, lambda i: (0, 0)),
        ],
        out_specs=[pl.BlockSpec((ODD_TM, w), lambda i: (i, 0)) for w in widths]
        + [pl.BlockSpec((ODD_TM, ODD_TAIL), lambda i: (i, 0))],
        out_shape=[jax.ShapeDtypeStruct((n, w), BF16) for w in widths]
        + [jax.ShapeDtypeStruct((n, ODD_TAIL), F32)],
        compiler_params=_cparams("parallel"),
        name="odd_in",
    )(x2, w_in)


DSA_TQ = 128
DSA_KC = 512


def _dsa_kernel(qi_ref, w_ref, kidx_ref, q_ref, k_ref, v_ref, o_ref, key_ref, dm_ref, *, topk):
    tq, kc = DSA_TQ, DSA_KC
    qblk = pl.program_id(1)
    t0 = qblk * tq
    nch = (t0 + tq + kc - 1) // kc

    lane = lax.broadcasted_iota(I32, (tq, LANES), 1)
    low = lane < HEAD_DIM
    row = lax.broadcasted_iota(I32, (tq, kc), 0)
    col = lax.broadcasted_iota(I32, (tq, kc), 1)
    t_pos = t0 + row
    vis_end = (t_pos // CHUNK + 1) * CHUNK

    qi = qi_ref[...]
    wv = w_ref[...] * (IDX_HEADS ** -0.5 * IDX_DIM ** -0.5)
    qm, cb = [], []
    for h in range(IDX_HEADS):
        blk = qi[:, (h // 2) * LANES:(h // 2 + 1) * LANES]
        qm.append(jnp.where(low if h % 2 == 0 else jnp.logical_not(low), blk, jnp.zeros_like(blk)))
        cb.append(wv[:, h:h + 1])

    def score_chunk(c, carry):
        kk = kidx_ref[pl.ds(pl.multiple_of(c * kc, kc), kc), :]
        acc = jnp.zeros((tq, kc), F32)
        for h in range(IDX_HEADS):
            acc = acc + cb[h] * jnp.maximum(_dot_nt(qm[h], kk), 0.0)
        acc = jnp.where(acc == 0.0, 0.0, acc)
        bits = pltpu.bitcast(acc, I32)
        key = jnp.where(bits < 0, bits ^ 0x7FFFFFFF, bits)
        key_ref[c] = jnp.where(c * kc + col < vis_end, key, INT_MIN)
        return carry

    lax.fori_loop(0, nch, score_chunk, 0)

    def count(pred_fn):
        def body(c, acc):
            return acc + jnp.sum(jnp.where(pred_fn(key_ref[c]), 1.0, 0.0), axis=1, keepdims=True)
        return lax.fori_loop(0, nch, body, jnp.zeros((tq, 1), F32))

    def bit_step(it, t_u):
        cand_u = t_u | jnp.left_shift(jnp.ones((tq, 1), I32), 31 - it)
        cand_s = cand_u ^ INT_MIN
        cnt = count(lambda kk: kk >= cand_s)
        return jnp.where(cnt >= topk, cand_u, t_u)

    t_u = lax.fori_loop(0, 32, bit_step, jnp.zeros((tq, 1), I32))
    thr = t_u ^ INT_MIN
    need = topk - count(lambda kk: kk > thr)

    rr = lax.broadcasted_iota(I32, (kc, kc), 0)
    cc = lax.broadcasted_iota(I32, (kc, kc), 1)
    prefix = (rr <= cc).astype(BF16)

    def select_chunk(c, seen):
        kk = key_ref[c]
        eq = kk == thr
        eqf = jnp.where(eq, 1.0, 0.0)
        rank = _dot(eqf.astype(BF16), prefix) + seen
        sel = ((kk > thr) | (eq & (rank <= need))) & (kk != INT_MIN)
        dist = jnp.abs(t_pos - (c * kc + col)).astype(F32)
        dm_ref[c] = jnp.where(sel, -dist, NEG_BIG)
        return seen + jnp.sum(eqf, axis=1, keepdims=True)

    lax.fori_loop(0, nch, select_chunk, jnp.zeros((tq, 1), F32))

    q = q_ref[...]
    hpg = C_HEADS_PER_GROUP
    for g in range(C_KV_GROUPS):
        qs = []
        for r in range(hpg):
            hd = g * hpg + r
            blk = q[:, (hd // 2) * LANES:(hd // 2 + 1) * LANES]
            qs.append(jnp.where(low if hd % 2 == 0 else jnp.logical_not(low), blk, jnp.zeros_like(blk))
                      * (HEAD_DIM ** -0.5))
        qg = jnp.concatenate(qs, axis=0)
        slopes = [2.0 ** (-8.0 * (g * hpg + r + 1) / C_HEADS) for r in range(hpg)]

        def attn_chunk(c, state, g=g, qg=qg, slopes=slopes):
            m, l, acc = state
            start = pl.multiple_of(c * kc, kc)
            kd = k_ref[g, pl.ds(start, kc), :]
            vd = v_ref[g, pl.ds(start, kc), :]
            s = _dot_nt(qg, kd)
            dm = dm_ref[c]
            lg = jnp.concatenate([s[r * tq:(r + 1) * tq] + slopes[r] * dm for r in range(hpg)], axis=0)
            m_new = jnp.maximum(m, jnp.max(lg, axis=1, keepdims=True))
            a = jnp.exp(m - m_new)
            p = jnp.exp(lg - m_new)
            l = l * a + jnp.sum(p, axis=1, keepdims=True)
            acc = acc * a + _dot(p.astype(BF16), vd)
            return m_new, l, acc

        init = (jnp.full((hpg * tq, 1), NEG_BIG * 4.0, F32), jnp.zeros((hpg * tq, 1), F32),
                jnp.zeros((hpg * tq, LANES), F32))
        m, l, acc = lax.fori_loop(0, nch, attn_chunk, init)
        o = acc / l
        for j in range(hpg // 2):
            oa = o[(2 * j) * tq:(2 * j + 1) * tq]
            ob = o[(2 * j + 1) * tq:(2 * j + 2) * tq]
            pair = g * (hpg // 2) + j
            o_ref[:, pair * LANES:(pair + 1) * LANES] = jnp.where(low, oa, ob).astype(BF16)


def _dsa(qidx3, widx3, kidx2, q3, kdup, vdup, topk):
    b, s, _ = q3.shape
    nch_max = s // DSA_KC
    return pl.pallas_call(
        functools.partial(_dsa_kernel, topk=topk),
        grid=(b, s // DSA_TQ),
        in_specs=[
            pl.BlockSpec((None, DSA_TQ, IDX_HEADS * IDX_DIM), lambda bi, qi: (bi, qi, 0)),
            pl.BlockSpec((None, DSA_TQ, IDX_HEADS), lambda bi, qi: (bi, qi, 0)),
            pl.BlockSpec((None, s, LANES), lambda bi, qi: (bi, 0, 0)),
            pl.BlockSpec((None, DSA_TQ, C_WIDTH), lambda bi, qi: (bi, qi, 0)),
            pl.BlockSpec((None, C_KV_GROUPS, s, LANES), lambda bi, qi: (bi, 0, 0, 0)),
            pl.BlockSpec((None, C_KV_GROUPS, s, LANES), lambda bi, qi: (bi, 0, 0, 0)),
        ],
        out_specs=pl.BlockSpec((None, DSA_TQ, C_WIDTH), lambda bi, qi: (bi, qi, 0)),
        out_shape=jax.ShapeDtypeStruct((b, s, C_WIDTH), BF16),
        scratch_shapes=[
            pltpu.VMEM((nch_max, DSA_TQ, DSA_KC), I32),
            pltpu.VMEM((nch_max, DSA_TQ, DSA_KC), F32),
        ],
        compiler_params=_cparams("parallel", "arbitrary"),
        name="dsa",
    )(qidx3, widx3, kidx2, q3, kdup, vdup)


def _row(v):
    return v.reshape(1, -1)


def _router_weights(w_router, b_router):
    pad = LANES - N_EXPERTS
    w = jnp.pad(w_router, ((0, 0), (0, pad)))
    hi = w.astype(BF16)
    lo = (w - hi.astype(F32)).astype(BF16)
    br = jnp.pad(b_router, (0, pad), constant_values=NEG_BIG).reshape(1, LANES)
    return hi, lo, br


def _channel_mixer(i, h, top_idx, gates, p2, w_gu, b_gu, w_down, b_down, ln2_g, ln2_b, w_ple, w_ple_gate, b_ple_gate):
    n = h.shape[0]
    dest, row_tok, row_gate, block_e = _moe_routing(top_idx, gates, n)
    xs = _gather_rows(h, row_tok)
    ys = _experts(xs, block_e, w_gu[i].astype(BF16), b_gu[i][:, None, :], w_down[i].astype(BF16),
                  b_down[i][:, None, :], row_gate[:, None])
    return _combine(dest, ys, h, p2, w_ple_gate[i].astype(BF16), _row(b_ple_gate[i]), w_ple[i].astype(BF16),
                    _row(ln2_g[i]), _row(ln2_b[i]))


def kernel(x, p, even_w_in, even_gmlp_ln_g, even_gmlp_ln_b, even_w_s, even_b_s, even_w_o, odd_w_in, odd_w_o, ln1_g, ln1_b, w_router, b_router, w_gu, b_gu, w_down, b_down, ln2_g, ln2_b, w_ple, w_ple_gate, b_ple_gate):
    b, s, d = x.shape
    n = b * s
    x2 = x.reshape(n, d)
    for i in range(DEPTH):
        j = i // 2
        wr_hi, wr_lo, br = _router_weights(w_router[i], b_router[i])
        if i % 2 == 0:
            bs_full = jnp.repeat(even_b_s[j].T, A_WIDTH // A_GROUPS, axis=1)
            a_out, qkv = _even_in(x2, even_w_in[j].astype(BF16), _row(even_gmlp_ln_g[j]), _row(even_gmlp_ln_b[j]),
                                  even_w_s[j], bs_full)
            b_out = _stick_breaking(qkv.reshape(b, s, -1)).reshape(n, B_WIDTH)
            w_o = even_w_o[j].astype(BF16)
            parts, weights = [a_out, b_out], [w_o[:A_WIDTH], w_o[A_WIDTH:]]
        else:
            w_in = jnp.pad(odd_w_in[j], ((0, 0), (0, C_WIDTH + 2 * C_KV_WIDTH + IDX_HEADS * IDX_DIM + ODD_TAIL
                                                  - odd_w_in.shape[2]))).astype(BF16)
            q, kv, qidx, tail = _odd_in(x2, w_in)
            kidx = tail[:, :IDX_DIM].astype(BF16).reshape(b, s, IDX_DIM)
            kidx2 = jnp.concatenate([kidx, kidx], axis=-1)
            widx = tail[:, IDX_DIM:IDX_DIM + IDX_HEADS].reshape(b, s, IDX_HEADS)
            kg = kv[:, :C_KV_WIDTH].reshape(b, s, C_KV_GROUPS, HEAD_DIM).transpose(0, 2, 1, 3)
            vg = kv[:, C_KV_WIDTH:].reshape(b, s, C_KV_GROUPS, HEAD_DIM).transpose(0, 2, 1, 3)
            kdup = jnp.concatenate([kg, kg], axis=-1)
            vdup = jnp.concatenate([vg, vg], axis=-1)
            topk = min(IDX_TOPK_MAX, s // 4)
            o = _dsa(qidx.reshape(b, s, -1), widx, kidx2, q.reshape(b, s, -1), kdup, vdup, topk)
            parts, weights = [o.reshape(n, C_WIDTH)], [odd_w_o[j].astype(BF16)]
        h, idx_pad, gate_pad = _post_mixer(parts, weights, x2, _row(ln1_g[i]), _row(ln1_b[i]), wr_hi, wr_lo, br)
        x2 = _channel_mixer(i, h, idx_pad[:, :TOP_K], gate_pad[:, :TOP_K], p[i].reshape(n, PLE_DIM),
                            w_gu, b_gu, w_down, b_down, ln2_g, ln2_b, w_ple, w_ple_gate, b_ple_gate)
    return x2.reshape(b, s, d)
```

```python
import functools

import jax
import jax.numpy as jnp
from jax import lax
from jax.experimental import pallas as pl
from jax.experimental.pallas import tpu as pltpu

F32 = jnp.float32
BF16 = jnp.bfloat16
I32 = jnp.int32

D_MODEL = 1024
DEPTH = 2
CHUNK = 64
A_WIDTH = 512
A_GROUPS = 8
GMLP_CHUNK = 128
HEAD_DIM = 64
B_HEADS = 8
B_WIDTH = 512
C_HEADS = 16
C_KV_GROUPS = 4
C_HEADS_PER_GROUP = 4
C_WIDTH = 1024
C_KV_WIDTH = 256
IDX_HEADS = 8
IDX_DIM = 64
IDX_TOPK_MAX = 256
N_EXPERTS = 32
TOP_K = 4
D_EXPERT = 1024
SWIGLU_ALPHA = 1.702
SWIGLU_LIMIT = 7.0
MOE_BLOCK = 256
PLE_DIM = 256
LN_EPS = 1e-5
DEEPNORM_ALPHA = (2 * DEPTH) ** 0.25

LANES = 128
VMEM_LIMIT = 48 * 1024 * 1024

NEG_BIG = -1e30
INT_MIN = -(2 ** 31)


def _cparams(*sem):
    return pltpu.CompilerParams(dimension_semantics=sem, vmem_limit_bytes=VMEM_LIMIT)


def _dot(a, b):
    return jnp.dot(a, b, preferred_element_type=F32)


def _dot_nt(a, b):
    return lax.dot_general(a, b, (((1,), (1,)), ((), ())), preferred_element_type=F32)


def _layer_norm(y, g, b):
    mu = jnp.mean(y, axis=-1, keepdims=True)
    yc = y - mu
    var = jnp.mean(yc * yc, axis=-1, keepdims=True)
    return yc * lax.rsqrt(var + LN_EPS) * g + b


def _sigmoid(x):
    return 1.0 / (1.0 + jnp.exp(-x))


EVEN_TM = 256


def _even_in_kernel(x_ref, w_ref, lng_ref, lnb_ref, ws_ref, bs_ref, a_ref, qkv_ref):
    tm = x_ref.shape[0]
    xb = x_ref[...].astype(BF16)
    za = _dot(xb, w_ref[:, : 2 * A_WIDTH])
    ga = 0.5 * za * (1.0 + jnp.tanh(0.7978845608028654 * (za + 0.044715 * (za * za * za))))
    u = ga[:, :A_WIDTH]
    vn = _layer_norm(ga[:, A_WIDTH:], lng_ref[...], lnb_ref[...]).astype(BF16)

    r = lax.broadcasted_iota(I32, (GMLP_CHUNK, GMLP_CHUNK), 0)
    c = lax.broadcasted_iota(I32, (GMLP_CHUNK, GMLP_CHUNK), 1)
    tril = c <= r
    w_s = [jnp.where(tril, ws_ref[g], 0.0).astype(BF16) for g in range(A_GROUPS)]
    low_half = lax.broadcasted_iota(I32, (GMLP_CHUNK, LANES), 1) < A_WIDTH // A_GROUPS
    for ci in range(tm // GMLP_CHUNK):
        rows = slice(ci * GMLP_CHUNK, (ci + 1) * GMLP_CHUNK)
        for p in range(A_GROUPS // 2):
            cols = slice(p * LANES, (p + 1) * LANES)
            vb = vn[rows, cols]
            sv = jnp.where(low_half, _dot(w_s[2 * p], vb), _dot(w_s[2 * p + 1], vb)) + bs_ref[:, cols]
            a_ref[rows, cols] = (u[rows, cols] * sv).astype(BF16)

    nq = qkv_ref.shape[1]
    for j in range(0, nq, 512):
        qkv_ref[:, j:j + 512] = _dot(xb, w_ref[:, 2 * A_WIDTH + j: 2 * A_WIDTH + j + 512]).astype(BF16)


def _even_in(x2, w_in, ln_g, ln_b, w_s, bs_full):
    n, d = x2.shape
    nq = w_in.shape[1] - 2 * A_WIDTH
    return pl.pallas_call(
        _even_in_kernel,
        grid=(n // EVEN_TM,),
        in_specs=[
            pl.BlockSpec((EVEN_TM, d), lambda i: (i, 0)),
            pl.BlockSpec(w_in.shape, lambda i: (0, 0)),
            pl.BlockSpec((1, A_WIDTH), lambda i: (0, 0)),
            pl.BlockSpec((1, A_WIDTH), lambda i: (0, 0)),
            pl.BlockSpec(w_s.shape, lambda i: (0, 0, 0)),
            pl.BlockSpec(bs_full.shape, lambda i: (0, 0)),
        ],
        out_specs=[
            pl.BlockSpec((EVEN_TM, A_WIDTH), lambda i: (i, 0)),
            pl.BlockSpec((EVEN_TM, nq), lambda i: (i, 0)),
        ],
        out_shape=[
            jax.ShapeDtypeStruct((n, A_WIDTH), BF16),
            jax.ShapeDtypeStruct((n, nq), BF16),
        ],
        compiler_params=_cparams("parallel"),
        name="even_in",
    )(x2, w_in, ln_g, ln_b, w_s, bs_full)


SB_T = 128
SB_UNDERFLOW = -104.0


def _sb_kernel(q_ref, k_ref, v_ref, o_ref):
    t = q_ref.shape[0]
    npair = q_ref.shape[1] // LANES
    qi = pl.program_id(1)
    low = lax.broadcasted_iota(I32, (t, LANES), 1) < HEAD_DIM
    q2s = []
    for p in range(npair):
        q = q_ref[:, p * LANES:(p + 1) * LANES]
        zero = jnp.zeros_like(q)
        q2s.append(jnp.concatenate([jnp.where(low, q, zero), jnp.where(low, zero, q)], axis=0) * (HEAD_DIM ** -0.5))
    r = lax.broadcasted_iota(I32, (t, t), 0)
    c = lax.broadcasted_iota(I32, (t, t), 1)
    suffix = (r > c).astype(BF16)
    r2 = lax.broadcasted_iota(I32, (2 * t, t), 0)
    c2 = lax.broadcasted_iota(I32, (2 * t, t), 1)
    strict = c2 < jnp.where(r2 >= t, r2 - t, r2)

    def block(j, carries, accs, diag):
        start = pl.multiple_of(j * t, t)
        new_c, new_a = [], []
        for p in range(npair):
            cols = slice(p * LANES, (p + 1) * LANES)
            kb = k_ref[pl.ds(start, t), cols]
            vb = v_ref[pl.ds(start, t), cols]
            s = _dot_nt(q2s[p], kb)
            log_not = -(jnp.maximum(s, 0.0) + jnp.log(1.0 + jnp.exp(-jnp.abs(s))))
            if diag:
                log_not = jnp.where(strict, log_not, 0.0)
            hi = log_not.astype(BF16)
            lo = (log_not - hi.astype(F32)).astype(BF16)
            after = _dot(hi, suffix) + _dot(lo, suffix) + carries[p]
            w = jnp.exp(s + log_not + after)
            if diag:
                w = jnp.where(strict, w, 0.0)
            new_a.append(accs[p] + _dot(w.astype(BF16), vb))
            new_c.append(carries[p] + jnp.sum(log_not, axis=1, keepdims=True))
        return tuple(new_c), tuple(new_a)

    carries, accs = block(qi, (jnp.zeros((2 * t, 1), F32),) * npair, (jnp.zeros((2 * t, LANES), F32),) * npair, True)

    def cond(st):
        return jnp.logical_and(st[0] < qi, jnp.max(functools.reduce(jnp.maximum, st[1])) > SB_UNDERFLOW)

    def body(st):
        it, carries, accs = st
        carries, accs = block(qi - 1 - it, carries, accs, False)
        return it + 1, carries, accs

    _, carries, accs = lax.while_loop(cond, body, (jnp.int32(0), carries, accs))
    for p in range(npair):
        o_ref[:, p * LANES:(p + 1) * LANES] = jnp.where(low, accs[p][:t], accs[p][t:]).astype(BF16)


def _stick_breaking(qkv3):
    b, s, _ = qkv3.shape
    return pl.pallas_call(
        _sb_kernel,
        grid=(b, s // SB_T),
        in_specs=[
            pl.BlockSpec((None, SB_T, B_WIDTH), lambda bi, qi: (bi, qi, 0)),
            pl.BlockSpec((None, s, B_WIDTH), lambda bi, qi: (bi, 0, 1)),
            pl.BlockSpec((None, s, B_WIDTH), lambda bi, qi: (bi, 0, 2)),
        ],
        out_specs=pl.BlockSpec((None, SB_T, B_WIDTH), lambda bi, qi: (bi, qi, 0)),
        out_shape=jax.ShapeDtypeStruct((b, s, B_WIDTH), BF16),
        compiler_params=_cparams("parallel", "arbitrary"),
        name="stick_breaking",
    )(qkv3, qkv3, qkv3)


POST_TM = 256


def _post_mixer_kernel(*refs, n_in):
    ins = refs[:n_in]
    ws = refs[n_in:2 * n_in]
    x_ref, g_ref, b_ref, wrh_ref, wrl_ref, br_ref, h_ref, idx_ref, gate_ref = refs[2 * n_in:]
    tm = x_ref.shape[0]
    mix = _dot(ins[0][...], ws[0][...])
    for a, w in zip(ins[1:], ws[1:]):
        mix = mix + _dot(a[...], w[...])
    h = _layer_norm(DEEPNORM_ALPHA * x_ref[...] + mix, g_ref[...], b_ref[...])
    h_ref[...] = h

    hh = h.astype(BF16)
    hl = (h - hh.astype(F32)).astype(BF16)
    logits = _dot(hh, wrh_ref[...]) + _dot(hl, wrh_ref[...]) + _dot(hh, wrl_ref[...]) + br_ref[...]

    lane = lax.broadcasted_iota(I32, (tm, LANES), 1)
    vals, idxs = [], []
    for _ in range(TOP_K):
        m = jnp.max(logits, axis=1, keepdims=True)
        i = jnp.min(jnp.where(logits == m, lane, LANES), axis=1, keepdims=True)
        vals.append(m)
        idxs.append(i)
        logits = jnp.where(lane == i, NEG_BIG * 2.0, logits)
    es = [jnp.exp(v - vals[0]) for v in vals]
    inv = 1.0 / (es[0] + es[1] + es[2] + es[3])
    idx_out = jnp.zeros((tm, LANES), I32)
    gate_out = jnp.zeros((tm, LANES), F32)
    for k in range(TOP_K):
        idx_out = jnp.where(lane == k, idxs[k], idx_out)
        gate_out = jnp.where(lane == k, es[k] * inv, gate_out)
    idx_ref[...] = idx_out
    gate_ref[...] = gate_out


def _post_mixer(parts, weights, x2, ln_g, ln_b, wr_hi, wr_lo, br):
    n, d = x2.shape
    n_in = len(parts)
    in_specs = [pl.BlockSpec((POST_TM, a.shape[1]), lambda i: (i, 0)) for a in parts]
    in_specs += [pl.BlockSpec(w.shape, lambda i: (0, 0)) for w in weights]
    in_specs += [
        pl.BlockSpec((POST_TM, d), lambda i: (i, 0)),
        pl.BlockSpec((1, d), lambda i: (0, 0)),
        pl.BlockSpec((1, d), lambda i: (0, 0)),
        pl.BlockSpec(wr_hi.shape, lambda i: (0, 0)),
        pl.BlockSpec(wr_lo.shape, lambda i: (0, 0)),
        pl.BlockSpec((1, LANES), lambda i: (0, 0)),
    ]
    return pl.pallas_call(
        functools.partial(_post_mixer_kernel, n_in=n_in),
        grid=(n // POST_TM,),
        in_specs=in_specs,
        out_specs=[
            pl.BlockSpec((POST_TM, d), lambda i: (i, 0)),
            pl.BlockSpec((POST_TM, LANES), lambda i: (i, 0)),
            pl.BlockSpec((POST_TM, LANES), lambda i: (i, 0)),
        ],
        out_shape=[
            jax.ShapeDtypeStruct((n, d), F32),
            jax.ShapeDtypeStruct((n, LANES), I32),
            jax.ShapeDtypeStruct((n, LANES), F32),
        ],
        compiler_params=_cparams("parallel"),
        name="post_mixer",
    )(*parts, *weights, x2, ln_g, ln_b, wr_hi, wr_lo, br)


def _row_copy(src_hbm, src_row, dst_ref, dst_row, sem):
    return pltpu.make_async_copy(src_hbm.at[pl.ds(src_row, 1), :], dst_ref.at[pl.ds(dst_row, 1), :], sem)


def _gather_rows_kernel(tok_ref, h_hbm, o_ref, sem):
    rows = o_ref.shape[0]

    def start(r, c):
        _row_copy(h_hbm, tok_ref[0, 0, r], o_ref, r, sem).start()
        return c

    lax.fori_loop(0, rows, start, 0, unroll=8)

    def wait(r, c):
        _row_copy(h_hbm, 0, o_ref, r, sem).wait()
        return c

    lax.fori_loop(0, rows, wait, 0, unroll=8)


def _gather_rows(h, row_tok):
    n_rows = row_tok.shape[0]
    d = h.shape[1]
    nb = n_rows // MOE_BLOCK
    return pl.pallas_call(
        _gather_rows_kernel,
        grid=(nb,),
        in_specs=[
            pl.BlockSpec((1, 1, MOE_BLOCK), lambda i: (i, 0, 0), memory_space=pltpu.SMEM),
            pl.BlockSpec(memory_space=pl.ANY),
        ],
        out_specs=pl.BlockSpec((MOE_BLOCK, d), lambda i: (i, 0)),
        out_shape=jax.ShapeDtypeStruct((n_rows, d), h.dtype),
        scratch_shapes=[pltpu.SemaphoreType.DMA],
        compiler_params=_cparams("arbitrary"),
        name="moe_gather",
    )(row_tok.reshape(nb, 1, MOE_BLOCK), h)


def _expert_kernel(be_ref, x_ref, wgu_ref, bgu_ref, wd_ref, bd_ref, g_ref, y_ref):
    del be_ref
    xb = x_ref[...].astype(BF16)
    gu = _dot(xb, wgu_ref[...]) + bgu_ref[...]
    gate = jnp.minimum(gu[:, :D_EXPERT], SWIGLU_LIMIT)
    up = jnp.clip(gu[:, D_EXPERT:], -SWIGLU_LIMIT, SWIGLU_LIMIT)
    act = gate * _sigmoid(SWIGLU_ALPHA * gate) * (up + 1.0)
    y = _dot(act.astype(BF16), wd_ref[...]) + bd_ref[...]
    y_ref[...] = y * g_ref[...]


def _experts(xs, block_e, w_gu, b_gu, w_down, b_down, row_gate):
    n_rows, d = xs.shape
    nb = n_rows // MOE_BLOCK
    return pl.pallas_call(
        _expert_kernel,
        grid_spec=pltpu.PrefetchScalarGridSpec(
            num_scalar_prefetch=1,
            grid=(nb,),
            in_specs=[
                pl.BlockSpec((MOE_BLOCK, d), lambda i, be: (i, 0)),
                pl.BlockSpec((None, d, 2 * D_EXPERT), lambda i, be: (be[i], 0, 0)),
                pl.BlockSpec((None, 1, 2 * D_EXPERT), lambda i, be: (be[i], 0, 0)),
                pl.BlockSpec((None, D_EXPERT, d), lambda i, be: (be[i], 0, 0)),
                pl.BlockSpec((None, 1, d), lambda i, be: (be[i], 0, 0)),
                pl.BlockSpec((MOE_BLOCK, 1), lambda i, be: (i, 0)),
            ],
            out_specs=pl.BlockSpec((MOE_BLOCK, d), lambda i, be: (i, 0)),
        ),
        out_shape=jax.ShapeDtypeStruct((n_rows, d), F32),
        compiler_params=_cparams("arbitrary"),
        name="moe_experts",
    )(block_e, xs, w_gu, b_gu, w_down, b_down, row_gate)


COMB_TM = 256


def _combine_kernel(dest_ref, ys_hbm, h_ref, p_ref, wpg_ref, bpg_ref, wple_ref, g_ref, b_ref, o_ref, buf, sem):
    tm = h_ref.shape[0]

    def start(r, c):
        for k in range(TOP_K):
            _row_copy(ys_hbm, dest_ref[0, 0, r * TOP_K + k], buf.at[k], r, sem).start()
        return c

    lax.fori_loop(0, tm, start, 0, unroll=4)

    def wait(r, c):
        for k in range(TOP_K):
            _row_copy(ys_hbm, 0, buf.at[k], r, sem).wait()
        return c

    lax.fori_loop(0, tm, wait, 0, unroll=4)

    moe = (buf[0] + buf[1]) + (buf[2] + buf[3])
    mid = DEEPNORM_ALPHA * h_ref[...] + moe
    gate = _sigmoid(_dot(mid.astype(BF16), wpg_ref[...]) + bpg_ref[...])
    ple = _dot(p_ref[...].astype(BF16), wple_ref[...])
    o_ref[...] = _layer_norm(mid + gate * ple, g_ref[...], b_ref[...])


def _combine(dest, ys, h, p2, w_pg, b_pg, w_ple, ln_g, ln_b):
    n, d = h.shape
    nt = n // COMB_TM
    return pl.pallas_call(
        _combine_kernel,
        grid=(nt,),
        in_specs=[
            pl.BlockSpec((1, 1, COMB_TM * TOP_K), lambda i: (i, 0, 0), memory_space=pltpu.SMEM),
            pl.BlockSpec(memory_space=pl.ANY),
            pl.BlockSpec((COMB_TM, d), lambda i: (i, 0)),
            pl.BlockSpec((COMB_TM, PLE_DIM), lambda i: (i, 0)),
            pl.BlockSpec(w_pg.shape, lambda i: (0, 0)),
            pl.BlockSpec((1, d), lambda i: (0, 0)),
            pl.BlockSpec(w_ple.shape, lambda i: (0, 0)),
            pl.BlockSpec((1, d), lambda i: (0, 0)),
            pl.BlockSpec((1, d), lambda i: (0, 0)),
        ],
        out_specs=pl.BlockSpec((COMB_TM, d), lambda i: (i, 0)),
        out_shape=jax.ShapeDtypeStruct((n, d), F32),
        scratch_shapes=[pltpu.VMEM((TOP_K, COMB_TM, d), F32), pltpu.SemaphoreType.DMA],
        compiler_params=_cparams("arbitrary"),
        name="moe_combine",
    )(dest.reshape(nt, 1, COMB_TM * TOP_K), ys, h, p2, w_pg, b_pg, w_ple, ln_g, ln_b)


def _moe_routing(top_idx, gates, n_tok):
    n_asg = n_tok * TOP_K
    n_blocks = n_asg // MOE_BLOCK + N_EXPERTS
    n_rows = n_blocks * MOE_BLOCK
    sel = (top_idx[:, :, None] == jnp.arange(N_EXPERTS, dtype=I32)[None, None, :]).any(axis=1).astype(I32)
    pos = jnp.cumsum(sel, axis=0) - sel
    counts = jnp.sum(sel, axis=0)
    padded = (counts + MOE_BLOCK - 1) // MOE_BLOCK * MOE_BLOCK
    pad_ends = jnp.cumsum(padded)
    pad_starts = pad_ends - padded
    dest = pad_starts[top_idx] + jnp.take_along_axis(pos, top_idx, axis=1)
    flat_dest = dest.reshape(-1)
    flat_tok = jnp.arange(n_asg, dtype=I32) // TOP_K
    row_tok = jnp.zeros((n_rows,), I32).at[flat_dest].set(flat_tok)
    row_gate = jnp.zeros((n_rows,), F32).at[flat_dest].set(gates.reshape(-1))
    block_e = jnp.minimum(
        jnp.searchsorted(pad_ends, jnp.arange(n_blocks, dtype=pad_ends.dtype) * MOE_BLOCK, side="right"),
        N_EXPERTS - 1).astype(I32)
    return dest.astype(I32), row_tok, row_gate, block_e


ODD_TM = 512
ODD_TAIL = 128


def _odd_in_kernel(x_ref, w_ref, q_ref, kv_ref, qi_ref, tail_ref):
    xb = x_ref[...].astype(BF16)
    off = 0
    for ref in (q_ref, kv_ref, qi_ref):
        n = ref.shape[1]
        for j in range(0, n, 512):
            ref[:, j:j + 512] = _dot(xb, w_ref[:, off + j: off + j + 512]).astype(ref.dtype)
        off += n
    tail_ref[...] = _dot(xb, w_ref[:, off:off + ODD_TAIL])


def _odd_in(x2, w_in):
    n, d = x2.shape
    widths = (C_WIDTH, 2 * C_KV_WIDTH, IDX_HEADS * IDX_DIM)
    return pl.pallas_call(
        _odd_in_kernel,
        grid=(n // ODD_TM,),
        in_specs=[
            pl.BlockSpec((ODD_TM, d), lambda i: (i, 0)),
            pl.BlockSpec(w_in.shape, lambda i: (0, 0)),
        ],
        out_specs=[pl.BlockSpec((ODD_TM, w), lambda i: (i, 0)) for w in widths]
        + [pl.BlockSpec((ODD_TM, ODD_TAIL), lambda i: (i, 0))],
        out_shape=[jax.ShapeDtypeStruct((n, w), BF16) for w in widths]
        + [jax.ShapeDtypeStruct((n, ODD_TAIL), F32)],
        compiler_params=_cparams("parallel"),
        name="odd_in",
    )(x2, w_in)


DSA_TQ = 128
DSA_KC = 512


def _dsa_kernel(qi_ref, w_ref, kidx_ref, q_ref, k_ref, v_ref, o_ref, key_ref, dm_ref, *, topk):
    tq, kc = DSA_TQ, DSA_KC
    qblk = pl.program_id(1)
    t0 = qblk * tq
    nch = (t0 + tq + kc - 1) // kc

    lane = lax.broadcasted_iota(I32, (tq, LANES), 1)
    low = lane < HEAD_DIM
    row = lax.broadcasted_iota(I32, (tq, kc), 0)
    col = lax.broadcasted_iota(I32, (tq, kc), 1)
    t_pos = t0 + row
    vis_end = (t_pos // CHUNK + 1) * CHUNK

    qi = qi_ref[...]
    wv = w_ref[...] * (IDX_HEADS ** -0.5 * IDX_DIM ** -0.5)
    qm, cb = [], []
    for h in range(IDX_HEADS):
        blk = qi[:, (h // 2) * LANES:(h // 2 + 1) * LANES]
        qm.append(jnp.where(low if h % 2 == 0 else jnp.logical_not(low), blk, jnp.zeros_like(blk)))
        cb.append(wv[:, h:h + 1])

    def score_chunk(c, carry):
        kk = kidx_ref[pl.ds(pl.multiple_of(c * kc, kc), kc), :]
        acc = jnp.zeros((tq, kc), F32)
        for h in range(IDX_HEADS):
            acc = acc + cb[h] * jnp.maximum(_dot_nt(qm[h], kk), 0.0)
        acc = jnp.where(acc == 0.0, 0.0, acc)
        bits = pltpu.bitcast(acc, I32)
        key = jnp.where(bits < 0, bits ^ 0x7FFFFFFF, bits)
        key_ref[c] = jnp.where(c * kc + col < vis_end, key, INT_MIN)
        return carry

    lax.fori_loop(0, nch, score_chunk, 0)

    def count(pred_fn):
        def body(c, acc):
            hit = jnp.where(pred_fn(key_ref[c]), 1.0, 0.0)
            for j in range(kc // LANES):
                acc = acc + hit[:, j * LANES:(j + 1) * LANES]
            return acc
        acc = lax.fori_loop(0, nch, body, jnp.zeros((tq, LANES), F32))
        return jnp.sum(acc, axis=1, keepdims=True)

    def bit_step(it, t_u):
        cand_u = t_u | jnp.left_shift(jnp.ones((tq, 1), I32), 31 - it)
        cand_s = cand_u ^ INT_MIN
        cnt = count(lambda kk: kk >= cand_s)
        return jnp.where(cnt >= topk, cand_u, t_u)

    t_u = lax.fori_loop(0, 32, bit_step, jnp.zeros((tq, 1), I32))
    thr = t_u ^ INT_MIN
    need = topk - count(lambda kk: kk > thr)

    rr = lax.broadcasted_iota(I32, (kc, kc), 0)
    cc = lax.broadcasted_iota(I32, (kc, kc), 1)
    prefix = (rr <= cc).astype(BF16)

    def select_chunk(c, seen):
        kk = key_ref[c]
        eq = kk == thr
        eqf = jnp.where(eq, 1.0, 0.0)
        rank = _dot(eqf.astype(BF16), prefix) + seen
        sel = ((kk > thr) | (eq & (rank <= need))) & (kk != INT_MIN)
        dist = jnp.abs(t_pos - (c * kc + col)).astype(F32)
        dm_ref[c] = jnp.where(sel, -dist, NEG_BIG)
        return seen + jnp.sum(eqf, axis=1, keepdims=True)

    lax.fori_loop(0, nch, select_chunk, jnp.zeros((tq, 1), F32))

    q = q_ref[...]
    hpg = C_HEADS_PER_GROUP
    qgs, slopes = [], []
    for g in range(C_KV_GROUPS):
        qs = []
        for r in range(hpg):
            hd = g * hpg + r
            blk = q[:, (hd // 2) * LANES:(hd // 2 + 1) * LANES]
            qs.append(jnp.where(low if hd % 2 == 0 else jnp.logical_not(low), blk, jnp.zeros_like(blk))
                      * (HEAD_DIM ** -0.5))
        qgs.append(jnp.concatenate(qs, axis=0))
        slopes.append([2.0 ** (-8.0 * (g * hpg + r + 1) / C_HEADS) for r in range(hpg)])

    def attn_chunk(c, state):
        start = pl.multiple_of(c * kc, kc)
        dm = dm_ref[c]
        new_state = []
        for g in range(C_KV_GROUPS):
            m, l, acc = state[g]
            kd = k_ref[g, pl.ds(start, kc), :]
            vd = v_ref[g, pl.ds(start, kc), :]
            s = _dot_nt(qgs[g], kd)
            lg = jnp.concatenate([s[r * tq:(r + 1) * tq] + slopes[g][r] * dm for r in range(hpg)], axis=0)
            m_new = jnp.maximum(m, jnp.max(lg, axis=1, keepdims=True))
            a = jnp.exp(m - m_new)
            p = jnp.exp(lg - m_new)
            l = l * a + jnp.sum(p, axis=1, keepdims=True)
            acc = acc * a + _dot(p.astype(BF16), vd)
            new_state.append((m_new, l, acc))
        return tuple(new_state)

    init = (jnp.full((hpg * tq, 1), NEG_BIG * 4.0, F32), jnp.zeros((hpg * tq, 1), F32),
            jnp.zeros((hpg * tq, LANES), F32))
    state = lax.fori_loop(0, nch, attn_chunk, (init,) * C_KV_GROUPS)
    for g in range(C_KV_GROUPS):
        _, l, acc = state[g]
        o = acc / l
        for j in range(hpg // 2):
            oa = o[(2 * j) * tq:(2 * j + 1) * tq]
            ob = o[(2 * j + 1) * tq:(2 * j + 2) * tq]
            pair = g * (hpg // 2) + j
            o_ref[:, pair * LANES:(pair + 1) * LANES] = jnp.where(low, oa, ob).astype(BF16)


def _dsa(qidx3, widx3, kidx2, q3, kdup, vdup, topk):
    b, s, _ = q3.shape
    nch_max = s // DSA_KC
    return pl.pallas_call(
        functools.partial(_dsa_kernel, topk=topk),
        grid=(b, s // DSA_TQ),
        in_specs=[
            pl.BlockSpec((None, DSA_TQ, IDX_HEADS * IDX_DIM), lambda bi, qi: (bi, qi, 0)),
            pl.BlockSpec((None, DSA_TQ, IDX_HEADS), lambda bi, qi: (bi, qi, 0)),
            pl.BlockSpec((None, s, LANES), lambda bi, qi: (bi, 0, 0)),
            pl.BlockSpec((None, DSA_TQ, C_WIDTH), lambda bi, qi: (bi, qi, 0)),
            pl.BlockSpec((None, C_KV_GROUPS, s, LANES), lambda bi, qi: (bi, 0, 0, 0)),
            pl.BlockSpec((None, C_KV_GROUPS, s, LANES), lambda bi, qi: (bi, 0, 0, 0)),
        ],
        out_specs=pl.BlockSpec((None, DSA_TQ, C_WIDTH), lambda bi, qi: (bi, qi, 0)),
        out_shape=jax.ShapeDtypeStruct((b, s, C_WIDTH), BF16),
        scratch_shapes=[
            pltpu.VMEM((nch_max, DSA_TQ, DSA_KC), I32),
            pltpu.VMEM((nch_max, DSA_TQ, DSA_KC), F32),
        ],
        compiler_params=_cparams("parallel", "arbitrary"),
        name="dsa",
    )(qidx3, widx3, kidx2, q3, kdup, vdup)


def _row(v):
    return v.reshape(1, -1)


def _router_weights(w_router, b_router):
    pad = LANES - N_EXPERTS
    w = jnp.pad(w_router, ((0, 0), (0, pad)))
    hi = w.astype(BF16)
    lo = (w - hi.astype(F32)).astype(BF16)
    br = jnp.pad(b_router, (0, pad), constant_values=NEG_BIG).reshape(1, LANES)
    return hi, lo, br


def _channel_mixer(i, h, top_idx, gates, p2, w_gu, b_gu, w_down, b_down, ln2_g, ln2_b, w_ple, w_ple_gate, b_ple_gate):
    n = h.shape[0]
    dest, row_tok, row_gate, block_e = _moe_routing(top_idx, gates, n)
    xs = _gather_rows(h, row_tok)
    ys = _experts(xs, block_e, w_gu[i].astype(BF16), b_gu[i][:, None, :], w_down[i].astype(BF16),
                  b_down[i][:, None, :], row_gate[:, None])
    return _combine(dest, ys, h, p2, w_ple_gate[i].astype(BF16), _row(b_ple_gate[i]), w_ple[i].astype(BF16),
                    _row(ln2_g[i]), _row(ln2_b[i]))


def kernel(x, p, even_w_in, even_gmlp_ln_g, even_gmlp_ln_b, even_w_s, even_b_s, even_w_o, odd_w_in, odd_w_o, ln1_g, ln1_b, w_router, b_router, w_gu, b_gu, w_down, b_down, ln2_g, ln2_b, w_ple, w_ple_gate, b_ple_gate):
    b, s, d = x.shape
    n = b * s
    x2 = x.reshape(n, d)
    for i in range(DEPTH):
        j = i // 2
        wr_hi, wr_lo, br = _router_weights(w_router[i], b_router[i])
        if i % 2 == 0:
            bs_full = jnp.repeat(even_b_s[j].T, A_WIDTH // A_GROUPS, axis=1)
            a_out, qkv = _even_in(x2, even_w_in[j].astype(BF16), _row(even_gmlp_ln_g[j]), _row(even_gmlp_ln_b[j]),
                                  even_w_s[j], bs_full)
            b_out = _stick_breaking(qkv.reshape(b, s, -1)).reshape(n, B_WIDTH)
            w_o = even_w_o[j].astype(BF16)
            parts, weights = [a_out, b_out], [w_o[:A_WIDTH], w_o[A_WIDTH:]]
        else:
            w_in = jnp.pad(odd_w_in[j], ((0, 0), (0, C_WIDTH + 2 * C_KV_WIDTH + IDX_HEADS * IDX_DIM + ODD_TAIL
                                                  - odd_w_in.shape[2]))).astype(BF16)
            q, kv, qidx, tail = _odd_in(x2, w_in)
            kidx = tail[:, :IDX_DIM].astype(BF16).reshape(b, s, IDX_DIM)
            kidx2 = jnp.concatenate([kidx, kidx], axis=-1)
            widx = tail[:, IDX_DIM:IDX_DIM + IDX_HEADS].reshape(b, s, IDX_HEADS)
            kg = kv[:, :C_KV_WIDTH].reshape(b, s, C_KV_GROUPS, HEAD_DIM).transpose(0, 2, 1, 3)
            vg = kv[:, C_KV_WIDTH:].reshape(b, s, C_KV_GROUPS, HEAD_DIM).transpose(0, 2, 1, 3)
            kdup = jnp.concatenate([kg, kg], axis=-1)
            vdup = jnp.concatenate([vg, vg], axis=-1)
            topk = min(IDX_TOPK_MAX, s // 4)
            o = _dsa(qidx.reshape(b, s, -1), widx, kidx2, q.reshape(b, s, -1), kdup, vdup, topk)
            parts, weights = [o.reshape(n, C_WIDTH)], [odd_w_o[j].astype(BF16)]
        h, idx_pad, gate_pad = _post_mixer(parts, weights, x2, _row(ln1_g[i]), _row(ln1_b[i]), wr_hi, wr_lo, br)
        x2 = _channel_mixer(i, h, idx_pad[:, :TOP_K], gate_pad[:, :TOP_K], p[i].reshape(n, PLE_DIM),
                            w_gu, b_gu, w_down, b_down, ln2_g, ln2_b, w_ple, w_ple_gate, b_ple_gate)
    return x2.reshape(b, s, d)
```

```python
import functools

import jax
import jax.numpy as jnp
from jax import lax
from jax.experimental import pallas as pl
from jax.experimental.pallas import tpu as pltpu

F32 = jnp.float32
BF16 = jnp.bfloat16
I32 = jnp.int32

D_MODEL = 1024
DEPTH = 2
CHUNK = 64
A_WIDTH = 512
A_GROUPS = 8
GMLP_CHUNK = 128
HEAD_DIM = 64
B_HEADS = 8
B_WIDTH = 512
C_HEADS = 16
C_KV_GROUPS = 4
C_HEADS_PER_GROUP = 4
C_WIDTH = 1024
C_KV_WIDTH = 256
IDX_HEADS = 8
IDX_DIM = 64
IDX_TOPK_MAX = 256
N_EXPERTS = 32
TOP_K = 4
D_EXPERT = 1024
SWIGLU_ALPHA = 1.702
SWIGLU_LIMIT = 7.0
MOE_BLOCK = 256
PLE_DIM = 256
LN_EPS = 1e-5
DEEPNORM_ALPHA = (2 * DEPTH) ** 0.25

LANES = 128
VMEM_LIMIT = 48 * 1024 * 1024

NEG_BIG = -1e30
INT_MIN = -(2 ** 31)


def _cparams(*sem):
    return pltpu.CompilerParams(dimension_semantics=sem, vmem_limit_bytes=VMEM_LIMIT)


def _dot(a, b):
    return jnp.dot(a, b, preferred_element_type=F32)


def _dot_nt(a, b):
    return lax.dot_general(a, b, (((1,), (1,)), ((), ())), preferred_element_type=F32)


def _layer_norm(y, g, b):
    mu = jnp.mean(y, axis=-1, keepdims=True)
    yc = y - mu
    var = jnp.mean(yc * yc, axis=-1, keepdims=True)
    return yc * lax.rsqrt(var + LN_EPS) * g + b


def _sigmoid(x):
    return 1.0 / (1.0 + jnp.exp(-x))


EVEN_TM = 256


def _even_in_kernel(x_ref, w_ref, lng_ref, lnb_ref, ws_ref, bs_ref, a_ref, qkv_ref):
    tm = x_ref.shape[0]
    xb = x_ref[...].astype(BF16)
    za = _dot(xb, w_ref[:, : 2 * A_WIDTH])
    ga = 0.5 * za * (1.0 + jnp.tanh(0.7978845608028654 * (za + 0.044715 * (za * za * za))))
    u = ga[:, :A_WIDTH]
    vn = _layer_norm(ga[:, A_WIDTH:], lng_ref[...], lnb_ref[...]).astype(BF16)

    r = lax.broadcasted_iota(I32, (GMLP_CHUNK, GMLP_CHUNK), 0)
    c = lax.broadcasted_iota(I32, (GMLP_CHUNK, GMLP_CHUNK), 1)
    tril = c <= r
    w_s = [jnp.where(tril, ws_ref[g], 0.0).astype(BF16) for g in range(A_GROUPS)]
    low_half = lax.broadcasted_iota(I32, (GMLP_CHUNK, LANES), 1) < A_WIDTH // A_GROUPS
    for ci in range(tm // GMLP_CHUNK):
        rows = slice(ci * GMLP_CHUNK, (ci + 1) * GMLP_CHUNK)
        for p in range(A_GROUPS // 2):
            cols = slice(p * LANES, (p + 1) * LANES)
            vb = vn[rows, cols]
            sv = jnp.where(low_half, _dot(w_s[2 * p], vb), _dot(w_s[2 * p + 1], vb)) + bs_ref[:, cols]
            a_ref[rows, cols] = (u[rows, cols] * sv).astype(BF16)

    nq = qkv_ref.shape[1]
    for j in range(0, nq, 512):
        qkv_ref[:, j:j + 512] = _dot(xb, w_ref[:, 2 * A_WIDTH + j: 2 * A_WIDTH + j + 512]).astype(BF16)


def _even_in(x2, w_in, ln_g, ln_b, w_s, bs_full):
    n, d = x2.shape
    nq = w_in.shape[1] - 2 * A_WIDTH
    return pl.pallas_call(
        _even_in_kernel,
        grid=(n // EVEN_TM,),
        in_specs=[
            pl.BlockSpec((EVEN_TM, d), lambda i: (i, 0)),
            pl.BlockSpec(w_in.shape, lambda i: (0, 0)),
            pl.BlockSpec((1, A_WIDTH), lambda i: (0, 0)),
            pl.BlockSpec((1, A_WIDTH), lambda i: (0, 0)),
            pl.BlockSpec(w_s.shape, lambda i: (0, 0, 0)),
            pl.BlockSpec(bs_full.shape, lambda i: (0, 0)),
        ],
        out_specs=[
            pl.BlockSpec((EVEN_TM, A_WIDTH), lambda i: (i, 0)),
            pl.BlockSpec((EVEN_TM, nq), lambda i: (i, 0)),
        ],
        out_shape=[
            jax.ShapeDtypeStruct((n, A_WIDTH), BF16),
            jax.ShapeDtypeStruct((n, nq), BF16),
        ],
        compiler_params=_cparams("parallel"),
        name="even_in",
    )(x2, w_in, ln_g, ln_b, w_s, bs_full)


SB_T = 128
SB_UNDERFLOW = -104.0


def _sb_kernel(q_ref, k_ref, v_ref, o_ref):
    t = q_ref.shape[0]
    npair = q_ref.shape[1] // LANES
    qi = pl.program_id(1)
    low = lax.broadcasted_iota(I32, (t, LANES), 1) < HEAD_DIM
    q2s = []
    for p in range(npair):
        q = q_ref[:, p * LANES:(p + 1) * LANES]
        zero = jnp.zeros_like(q)
        q2s.append(jnp.concatenate([jnp.where(low, q, zero), jnp.where(low, zero, q)], axis=0) * (HEAD_DIM ** -0.5))
    r = lax.broadcasted_iota(I32, (t, t), 0)
    c = lax.broadcasted_iota(I32, (t, t), 1)
    suffix = (r > c).astype(BF16)
    r2 = lax.broadcasted_iota(I32, (2 * t, t), 0)
    c2 = lax.broadcasted_iota(I32, (2 * t, t), 1)
    strict = c2 < jnp.where(r2 >= t, r2 - t, r2)

    def block(j, carries, accs, diag):
        start = pl.multiple_of(j * t, t)
        new_c, new_a = [], []
        for p in range(npair):
            cols = slice(p * LANES, (p + 1) * LANES)
            kb = k_ref[pl.ds(start, t), cols]
            vb = v_ref[pl.ds(start, t), cols]
            s = _dot_nt(q2s[p], kb)
            log_not = -(jnp.maximum(s, 0.0) + jnp.log(1.0 + jnp.exp(-jnp.abs(s))))
            if diag:
                log_not = jnp.where(strict, log_not, 0.0)
            hi = log_not.astype(BF16)
            lo = (log_not - hi.astype(F32)).astype(BF16)
            after = _dot(hi, suffix) + _dot(lo, suffix) + carries[p]
            w = jnp.exp(s + log_not + after)
            if diag:
                w = jnp.where(strict, w, 0.0)
            new_a.append(accs[p] + _dot(w.astype(BF16), vb))
            new_c.append(carries[p] + jnp.sum(log_not, axis=1, keepdims=True))
        return tuple(new_c), tuple(new_a)

    carries, accs = block(qi, (jnp.zeros((2 * t, 1), F32),) * npair, (jnp.zeros((2 * t, LANES), F32),) * npair, True)

    def cond(st):
        return jnp.logical_and(st[0] < qi, jnp.max(functools.reduce(jnp.maximum, st[1])) > SB_UNDERFLOW)

    def body(st):
        it, carries, accs = st
        carries, accs = block(qi - 1 - it, carries, accs, False)
        return it + 1, carries, accs

    _, carries, accs = lax.while_loop(cond, body, (jnp.int32(0), carries, accs))
    for p in range(npair):
        o_ref[:, p * LANES:(p + 1) * LANES] = jnp.where(low, accs[p][:t], accs[p][t:]).astype(BF16)


def _stick_breaking(qkv3):
    b, s, _ = qkv3.shape
    return pl.pallas_call(
        _sb_kernel,
        grid=(b, s // SB_T),
        in_specs=[
            pl.BlockSpec((None, SB_T, B_WIDTH), lambda bi, qi: (bi, qi, 0)),
            pl.BlockSpec((None, s, B_WIDTH), lambda bi, qi: (bi, 0, 1)),
            pl.BlockSpec((None, s, B_WIDTH), lambda bi, qi: (bi, 0, 2)),
        ],
        out_specs=pl.BlockSpec((None, SB_T, B_WIDTH), lambda bi, qi: (bi, qi, 0)),
        out_shape=jax.ShapeDtypeStruct((b, s, B_WIDTH), BF16),
        compiler_params=_cparams("parallel", "arbitrary"),
        name="stick_breaking",
    )(qkv3, qkv3, qkv3)


POST_TM = 256


def _post_mixer_kernel(*refs, n_in):
    ins = refs[:n_in]
    ws = refs[n_in:2 * n_in]
    x_ref, g_ref, b_ref, wrh_ref, wrl_ref, br_ref, h_ref, route_ref, gate_ref, counts_ref, cnt_ref = refs[2 * n_in:]
    tm = x_ref.shape[0]

    @pl.when(pl.program_id(0) == 0)
    def _():
        cnt_ref[...] = jnp.zeros_like(cnt_ref)

    mix = _dot(ins[0][...], ws[0][...])
    for a, w in zip(ins[1:], ws[1:]):
        mix = mix + _dot(a[...], w[...])
    h = _layer_norm(DEEPNORM_ALPHA * x_ref[...] + mix, g_ref[...], b_ref[...])
    h_ref[...] = h

    hh = h.astype(BF16)
    hl = (h - hh.astype(F32)).astype(BF16)
    logits = _dot(hh, wrh_ref[...]) + _dot(hl, wrh_ref[...]) + _dot(hh, wrl_ref[...]) + br_ref[...]

    lane = lax.broadcasted_iota(I32, (tm, LANES), 1)
    vals, idxs = [], []
    for _ in range(TOP_K):
        m = jnp.max(logits, axis=1, keepdims=True)
        i = jnp.min(jnp.where(logits == m, lane, LANES), axis=1, keepdims=True)
        vals.append(m)
        idxs.append(i)
        logits = jnp.where(lane == i, NEG_BIG * 2.0, logits)
    es = [jnp.exp(v - vals[0]) for v in vals]
    inv = 1.0 / (es[0] + es[1] + es[2] + es[3])

    sel = jnp.zeros((tm, LANES), F32)
    for k in range(TOP_K):
        sel = jnp.where(lane == idxs[k], 1.0, sel)
    r = lax.broadcasted_iota(I32, (tm, tm), 0)
    c = lax.broadcasted_iota(I32, (tm, tm), 1)
    incl = _dot((c <= r).astype(BF16), sel.astype(BF16))
    before = incl - sel + cnt_ref[...]
    cnt_ref[...] = cnt_ref[...] + incl[tm - 1:tm, :]
    counts_ref[...] = cnt_ref[...]

    route = jnp.zeros((tm, LANES), I32)
    gate_out = jnp.zeros((tm, LANES), F32)
    for k in range(TOP_K):
        pos = jnp.sum(jnp.where(lane == idxs[k], before, 0.0), axis=1, keepdims=True).astype(I32)
        route = jnp.where(lane == k, idxs[k], route)
        route = jnp.where(lane == TOP_K + k, pos, route)
        gate_out = jnp.where(lane == k, es[k] * inv, gate_out)
    route_ref[...] = route
    gate_ref[...] = gate_out


def _post_mixer(parts, weights, x2, ln_g, ln_b, wr_hi, wr_lo, br):
    n, d = x2.shape
    n_in = len(parts)
    in_specs = [pl.BlockSpec((POST_TM, a.shape[1]), lambda i: (i, 0)) for a in parts]
    in_specs += [pl.BlockSpec(w.shape, lambda i: (0, 0)) for w in weights]
    in_specs += [
        pl.BlockSpec((POST_TM, d), lambda i: (i, 0)),
        pl.BlockSpec((1, d), lambda i: (0, 0)),
        pl.BlockSpec((1, d), lambda i: (0, 0)),
        pl.BlockSpec(wr_hi.shape, lambda i: (0, 0)),
        pl.BlockSpec(wr_lo.shape, lambda i: (0, 0)),
        pl.BlockSpec((1, LANES), lambda i: (0, 0)),
    ]
    return pl.pallas_call(
        functools.partial(_post_mixer_kernel, n_in=n_in),
        grid=(n // POST_TM,),
        in_specs=in_specs,
        out_specs=[
            pl.BlockSpec((POST_TM, d), lambda i: (i, 0)),
            pl.BlockSpec((POST_TM, LANES), lambda i: (i, 0)),
            pl.BlockSpec((POST_TM, LANES), lambda i: (i, 0)),
            pl.BlockSpec((1, LANES), lambda i: (0, 0)),
        ],
        out_shape=[
            jax.ShapeDtypeStruct((n, d), F32),
            jax.ShapeDtypeStruct((n, LANES), I32),
            jax.ShapeDtypeStruct((n, LANES), F32),
            jax.ShapeDtypeStruct((1, LANES), F32),
        ],
        scratch_shapes=[pltpu.VMEM((1, LANES), F32)],
        compiler_params=_cparams("arbitrary"),
        name="post_mixer",
    )(*parts, *weights, x2, ln_g, ln_b, wr_hi, wr_lo, br)


def _row_copy(src_hbm, src_row, dst_ref, dst_row, sem):
    return pltpu.make_async_copy(src_hbm.at[pl.ds(src_row, 1), :], dst_ref.at[pl.ds(dst_row, 1), :], sem)


DISP_TM = 256


def _dispatch_kernel(dest_ref, fill_ref, h_ref, xs_hbm, zero_ref, sem):
    tm = h_ref.shape[0]

    def start(r, c):
        for k in range(TOP_K):
            pltpu.make_async_copy(h_ref.at[pl.ds(r, 1), :], xs_hbm.at[pl.ds(dest_ref[0, 0, r * TOP_K + k], 1), :],
                                  sem).start()
        return c

    lax.fori_loop(0, tm, start, 0, unroll=4)

    def wait(r, c):
        for k in range(TOP_K):
            pltpu.make_async_copy(h_ref.at[pl.ds(r, 1), :], xs_hbm.at[pl.ds(0, 1), :], sem).wait()
        return c

    lax.fori_loop(0, tm, wait, 0, unroll=4)

    @pl.when(pl.program_id(0) == pl.num_programs(0) - 1)
    def _():
        zero_ref[...] = jnp.zeros_like(zero_ref)

        def per_expert(e, c):
            first = fill_ref[e]
            n_fill = fill_ref[N_EXPERTS + e]

            def fill(j, c2):
                pltpu.make_async_copy(zero_ref.at[pl.ds(0, 1), :], xs_hbm.at[pl.ds(first + j, 1), :], sem).start()
                return c2

            lax.fori_loop(0, n_fill, fill, 0)

            def fill_wait(j, c2):
                pltpu.make_async_copy(zero_ref.at[pl.ds(0, 1), :], xs_hbm.at[pl.ds(0, 1), :], sem).wait()
                return c2

            lax.fori_loop(0, n_fill, fill_wait, 0)
            return c

        lax.fori_loop(0, N_EXPERTS, per_expert, 0)

        def tail_copy(blk):
            return pltpu.make_async_copy(zero_ref, xs_hbm.at[pl.ds(pl.multiple_of(blk * MOE_BLOCK, MOE_BLOCK), MOE_BLOCK), :], sem)

        n_valid = fill_ref[2 * N_EXPERTS]
        n_blocks = xs_hbm.shape[0] // MOE_BLOCK

        def tail(blk, c):
            tail_copy(blk).start()
            return c

        lax.fori_loop(n_valid, n_blocks, tail, 0)

        def tail_wait(blk, c):
            tail_copy(blk).wait()
            return c

        lax.fori_loop(n_valid, n_blocks, tail_wait, 0)


def _dispatch(dest, fill, h, n_rows):
    n, d = h.shape
    nt = n // DISP_TM
    return pl.pallas_call(
        _dispatch_kernel,
        grid=(nt,),
        in_specs=[
            pl.BlockSpec((1, 1, DISP_TM * TOP_K), lambda i: (i, 0, 0), memory_space=pltpu.SMEM),
            pl.BlockSpec(memory_space=pltpu.SMEM),
            pl.BlockSpec((DISP_TM, d), lambda i: (i, 0)),
        ],
        out_specs=pl.BlockSpec(memory_space=pl.ANY),
        out_shape=jax.ShapeDtypeStruct((n_rows, d), h.dtype),
        scratch_shapes=[pltpu.VMEM((MOE_BLOCK, d), h.dtype), pltpu.SemaphoreType.DMA],
        compiler_params=_cparams("arbitrary"),
        name="moe_dispatch",
    )(dest.reshape(nt, 1, DISP_TM * TOP_K), fill, h)


def _expert_kernel(be_ref, nv_ref, x_ref, wgu_ref, bgu_ref, wd_ref, bd_ref, y_ref):
    del be_ref

    @pl.when(pl.program_id(0) < nv_ref[0])
    def _():
        xb = x_ref[...].astype(BF16)
        gu = _dot(xb, wgu_ref[...]) + bgu_ref[...]
        gate = jnp.minimum(gu[:, :D_EXPERT], SWIGLU_LIMIT)
        up = jnp.clip(gu[:, D_EXPERT:], -SWIGLU_LIMIT, SWIGLU_LIMIT)
        act = gate * _sigmoid(SWIGLU_ALPHA * gate) * (up + 1.0)
        y_ref[...] = _dot(act.astype(BF16), wd_ref[...]) + bd_ref[...]

    @pl.when(pl.program_id(0) >= nv_ref[0])
    def _():
        y_ref[...] = jnp.zeros_like(y_ref)


def _experts(xs, block_e, n_valid, w_gu, b_gu, w_down, b_down):
    n_rows, d = xs.shape
    nb = n_rows // MOE_BLOCK

    def rows(i, be, nv):
        return (i, 0)

    return pl.pallas_call(
        _expert_kernel,
        grid_spec=pltpu.PrefetchScalarGridSpec(
            num_scalar_prefetch=2,
            grid=(nb,),
            in_specs=[
                pl.BlockSpec((MOE_BLOCK, d), rows),
                pl.BlockSpec((None, d, 2 * D_EXPERT), lambda i, be, nv: (be[i], 0, 0)),
                pl.BlockSpec((None, 1, 2 * D_EXPERT), lambda i, be, nv: (be[i], 0, 0)),
                pl.BlockSpec((None, D_EXPERT, d), lambda i, be, nv: (be[i], 0, 0)),
                pl.BlockSpec((None, 1, d), lambda i, be, nv: (be[i], 0, 0)),
            ],
            out_specs=pl.BlockSpec((MOE_BLOCK, d), rows),
        ),
        out_shape=jax.ShapeDtypeStruct((n_rows, d), F32),
        compiler_params=_cparams("arbitrary"),
        name="moe_experts",
    )(block_e, n_valid, xs, w_gu, b_gu, w_down, b_down)


COMB_TM = 256


def _combine_kernel(dest_ref, ys_hbm, gate_ref, h_ref, p_ref, wpg_ref, bpg_ref, wple_ref, g_ref, b_ref, o_ref, buf, sem):
    tm = h_ref.shape[0]

    def start(r, c):
        for k in range(TOP_K):
            _row_copy(ys_hbm, dest_ref[0, 0, r * TOP_K + k], buf.at[k], r, sem).start()
        return c

    lax.fori_loop(0, tm, start, 0, unroll=4)

    def wait(r, c):
        for k in range(TOP_K):
            _row_copy(ys_hbm, 0, buf.at[k], r, sem).wait()
        return c

    lax.fori_loop(0, tm, wait, 0, unroll=4)

    gates = gate_ref[...]
    moe = buf[0] * gates[:, 0:1]
    for k in range(1, TOP_K):
        moe = moe + buf[k] * gates[:, k:k + 1]
    mid = DEEPNORM_ALPHA * h_ref[...] + moe
    gate = _sigmoid(_dot(mid.astype(BF16), wpg_ref[...]) + bpg_ref[...])
    ple = _dot(p_ref[...].astype(BF16), wple_ref[...])
    o_ref[...] = _layer_norm(mid + gate * ple, g_ref[...], b_ref[...])


def _combine(dest, ys, gates, h, p2, w_pg, b_pg, w_ple, ln_g, ln_b):
    n, d = h.shape
    nt = n // COMB_TM
    return pl.pallas_call(
        _combine_kernel,
        grid=(nt,),
        in_specs=[
            pl.BlockSpec((1, 1, COMB_TM * TOP_K), lambda i: (i, 0, 0), memory_space=pltpu.SMEM),
            pl.BlockSpec(memory_space=pl.ANY),
            pl.BlockSpec((COMB_TM, LANES), lambda i: (i, 0)),
            pl.BlockSpec((COMB_TM, d), lambda i: (i, 0)),
            pl.BlockSpec((COMB_TM, PLE_DIM), lambda i: (i, 0)),
            pl.BlockSpec(w_pg.shape, lambda i: (0, 0)),
            pl.BlockSpec((1, d), lambda i: (0, 0)),
            pl.BlockSpec(w_ple.shape, lambda i: (0, 0)),
            pl.BlockSpec((1, d), lambda i: (0, 0)),
            pl.BlockSpec((1, d), lambda i: (0, 0)),
        ],
        out_specs=pl.BlockSpec((COMB_TM, d), lambda i: (i, 0)),
        out_shape=jax.ShapeDtypeStruct((n, d), F32),
        scratch_shapes=[pltpu.VMEM((TOP_K, COMB_TM, d), F32), pltpu.SemaphoreType.DMA],
        compiler_params=_cparams("arbitrary"),
        name="moe_combine",
    )(dest.reshape(nt, 1, COMB_TM * TOP_K), ys, gates, h, p2, w_pg, b_pg, w_ple, ln_g, ln_b)


def _moe_layout(route, counts, n_tok):
    n_blocks = n_tok * TOP_K // MOE_BLOCK + N_EXPERTS
    top_idx = route[:, :TOP_K]
    pos = route[:, TOP_K:2 * TOP_K]
    counts = counts[0, :N_EXPERTS].astype(I32)
    padded = (counts + MOE_BLOCK - 1) // MOE_BLOCK * MOE_BLOCK
    pad_ends = jnp.cumsum(padded)
    pad_starts = pad_ends - padded
    dest = pad_starts[top_idx] + pos
    n_valid = (pad_ends[-1:] // MOE_BLOCK).astype(I32)
    fill = jnp.concatenate([pad_starts + counts, padded - counts, n_valid]).astype(I32)
    block_start = jnp.arange(n_blocks, dtype=I32) * MOE_BLOCK
    block_e = jnp.minimum(jnp.sum((pad_ends[None, :] <= block_start[:, None]).astype(I32), axis=1), N_EXPERTS - 1)
    return dest.astype(I32), fill, block_e.astype(I32), n_valid, n_blocks * MOE_BLOCK


ODD_TM = 512
ODD_TAIL = 128


def _odd_in_kernel(x_ref, w_ref, q_ref, kv_ref, qi_ref, tail_ref):
    xb = x_ref[...].astype(BF16)
    off = 0
    for ref in (q_ref, kv_ref, qi_ref):
        n = ref.shape[1]
        for j in range(0, n, 512):
            ref[:, j:j + 512] = _dot(xb, w_ref[:, off + j: off + j + 512]).astype(ref.dtype)
        off += n
    tail_ref[...] = _dot(xb, w_ref[:, off:off + ODD_TAIL])


def _odd_in(x2, w_in):
    n, d = x2.shape
    widths = (C_WIDTH, 2 * C_KV_WIDTH, IDX_HEADS * IDX_DIM)
    return pl.pallas_call(
        _odd_in_kernel,
        grid=(n // ODD_TM,),
        in_specs=[
            pl.BlockSpec((ODD_TM, d), lambda i: (i, 0)),
            pl.BlockSpec(w_in.shape, lambda i: (0, 0)),
        ],
        out_specs=[pl.BlockSpec((ODD_TM, w), lambda i: (i, 0)) for w in widths]
        + [pl.BlockSpec((ODD_TM, ODD_TAIL), lambda i: (i, 0))],
        out_shape=[jax.ShapeDtypeStruct((n, w), BF16) for w in widths]
        + [jax.ShapeDtypeStruct((n, ODD_TAIL), F32)],
        compiler_params=_cparams("parallel"),
        name="odd_in",
    )(x2, w_in)


DSA_TQ = 128
DSA_KC = 512


def _dsa_kernel(qi_ref, w_ref, kidx_ref, q_ref, k_ref, v_ref, o_ref, key_ref, dm_ref, *, topk):
    tq, kc = DSA_TQ, DSA_KC
    qblk = pl.program_id(1)
    t0 = qblk * tq
    nch = (t0 + tq + kc - 1) // kc

    lane = lax.broadcasted_iota(I32, (tq, LANES), 1)
    low = lane < HEAD_DIM
    row = lax.broadcasted_iota(I32, (tq, kc), 0)
    col = lax.broadcasted_iota(I32, (tq, kc), 1)
    t_pos = t0 + row
    vis_end = (t_pos // CHUNK + 1) * CHUNK

    qi = qi_ref[...]
    wv = w_ref[...] * (IDX_HEADS ** -0.5 * IDX_DIM ** -0.5)
    qm, cb = [], []
    for h in range(IDX_HEADS):
        blk = qi[:, (h // 2) * LANES:(h // 2 + 1) * LANES]
        qm.append(jnp.where(low if h % 2 == 0 else jnp.logical_not(low), blk, jnp.zeros_like(blk)))
        cb.append(wv[:, h:h + 1])

    def score_chunk(c, carry):
        kk = kidx_ref[pl.ds(pl.multiple_of(c * kc, kc), kc), :]
        acc = jnp.zeros((tq, kc), F32)
        for h in range(IDX_HEADS):
            acc = acc + cb[h] * jnp.maximum(_dot_nt(qm[h], kk), 0.0)
        acc = jnp.where(acc == 0.0, 0.0, acc)
        bits = pltpu.bitcast(acc, I32)
        key = jnp.where(bits < 0, bits ^ 0x7FFFFFFF, bits)
        key_ref[c] = jnp.where(c * kc + col < vis_end, key, INT_MIN)
        return carry

    lax.fori_loop(0, nch, score_chunk, 0)

    def count(pred_fn):
        def body(c, acc):
            hit = jnp.where(pred_fn(key_ref[c]), 1.0, 0.0)
            for j in range(kc // LANES):
                acc = acc + hit[:, j * LANES:(j + 1) * LANES]
            return acc
        acc = lax.fori_loop(0, nch, body, jnp.zeros((tq, LANES), F32))
        return jnp.sum(acc, axis=1, keepdims=True)

    def bit_step(it, t_u):
        cand_u = t_u | jnp.left_shift(jnp.ones((tq, 1), I32), 31 - it)
        cand_s = cand_u ^ INT_MIN
        cnt = count(lambda kk: kk >= cand_s)
        return jnp.where(cnt >= topk, cand_u, t_u)

    t_u = lax.fori_loop(0, 32, bit_step, jnp.zeros((tq, 1), I32))
    thr = t_u ^ INT_MIN
    need = topk - count(lambda kk: kk > thr)

    rr = lax.broadcasted_iota(I32, (kc, kc), 0)
    cc = lax.broadcasted_iota(I32, (kc, kc), 1)
    prefix = (rr <= cc).astype(BF16)

    def select_chunk(c, seen):
        kk = key_ref[c]
        eq = kk == thr
        eqf = jnp.where(eq, 1.0, 0.0)
        rank = _dot(eqf.astype(BF16), prefix) + seen
        sel = ((kk > thr) | (eq & (rank <= need))) & (kk != INT_MIN)
        dist = jnp.abs(t_pos - (c * kc + col)).astype(F32)
        dm_ref[c] = jnp.where(sel, -dist, NEG_BIG)
        return seen + jnp.sum(eqf, axis=1, keepdims=True)

    lax.fori_loop(0, nch, select_chunk, jnp.zeros((tq, 1), F32))

    q = q_ref[...]
    hpg = C_HEADS_PER_GROUP
    qgs, slopes = [], []
    for g in range(C_KV_GROUPS):
        qs = []
        for r in range(hpg):
            hd = g * hpg + r
            blk = q[:, (hd // 2) * LANES:(hd // 2 + 1) * LANES]
            qs.append(jnp.where(low if hd % 2 == 0 else jnp.logical_not(low), blk, jnp.zeros_like(blk))
                      * (HEAD_DIM ** -0.5))
        qgs.append(jnp.concatenate(qs, axis=0))
        slopes.append([2.0 ** (-8.0 * (g * hpg + r + 1) / C_HEADS) for r in range(hpg)])

    def attn_chunk(c, state):
        start = pl.multiple_of(c * kc, kc)
        dm = dm_ref[c]
        new_state = []
        for g in range(C_KV_GROUPS):
            m, l, acc = state[g]
            kd = k_ref[g, pl.ds(start, kc), :]
            vd = v_ref[g, pl.ds(start, kc), :]
            s = _dot_nt(qgs[g], kd)
            lg = jnp.concatenate([s[r * tq:(r + 1) * tq] + slopes[g][r] * dm for r in range(hpg)], axis=0)
            m_new = jnp.maximum(m, jnp.max(lg, axis=1, keepdims=True))
            a = jnp.exp(m - m_new)
            p = jnp.exp(lg - m_new)
            l = l * a + jnp.sum(p, axis=1, keepdims=True)
            acc = acc * a + _dot(p.astype(BF16), vd)
            new_state.append((m_new, l, acc))
        return tuple(new_state)

    init = (jnp.full((hpg * tq, 1), NEG_BIG * 4.0, F32), jnp.zeros((hpg * tq, 1), F32),
            jnp.zeros((hpg * tq, LANES), F32))
    state = lax.fori_loop(0, nch, attn_chunk, (init,) * C_KV_GROUPS)
    for g in range(C_KV_GROUPS):
        _, l, acc = state[g]
        o = acc / l
        for j in range(hpg // 2):
            oa = o[(2 * j) * tq:(2 * j + 1) * tq]
            ob = o[(2 * j + 1) * tq:(2 * j + 2) * tq]
            pair = g * (hpg // 2) + j
            o_ref[:, pair * LANES:(pair + 1) * LANES] = jnp.where(low, oa, ob).astype(BF16)


def _dsa(qidx3, widx3, kidx2, q3, kdup, vdup, topk):
    b, s, _ = q3.shape
    nch_max = s // DSA_KC
    return pl.pallas_call(
        functools.partial(_dsa_kernel, topk=topk),
        grid=(b, s // DSA_TQ),
        in_specs=[
            pl.BlockSpec((None, DSA_TQ, IDX_HEADS * IDX_DIM), lambda bi, qi: (bi, qi, 0)),
            pl.BlockSpec((None, DSA_TQ, IDX_HEADS), lambda bi, qi: (bi, qi, 0)),
            pl.BlockSpec((None, s, LANES), lambda bi, qi: (bi, 0, 0)),
            pl.BlockSpec((None, DSA_TQ, C_WIDTH), lambda bi, qi: (bi, qi, 0)),
            pl.BlockSpec((None, C_KV_GROUPS, s, LANES), lambda bi, qi: (bi, 0, 0, 0)),
            pl.BlockSpec((None, C_KV_GROUPS, s, LANES), lambda bi, qi: (bi, 0, 0, 0)),
        ],
        out_specs=pl.BlockSpec((None, DSA_TQ, C_WIDTH), lambda bi, qi: (bi, qi, 0)),
        out_shape=jax.ShapeDtypeStruct((b, s, C_WIDTH), BF16),
        scratch_shapes=[
            pltpu.VMEM((nch_max, DSA_TQ, DSA_KC), I32),
            pltpu.VMEM((nch_max, DSA_TQ, DSA_KC), F32),
        ],
        compiler_params=_cparams("parallel", "arbitrary"),
        name="dsa",
    )(qidx3, widx3, kidx2, q3, kdup, vdup)


def _row(v):
    return v.reshape(1, -1)


def _router_weights(w_router, b_router):
    pad = LANES - N_EXPERTS
    w = jnp.pad(w_router, ((0, 0), (0, pad)))
    hi = w.astype(BF16)
    lo = (w - hi.astype(F32)).astype(BF16)
    br = jnp.pad(b_router, (0, pad), constant_values=NEG_BIG).reshape(1, LANES)
    return hi, lo, br


def _channel_mixer(i, h, route, gates, counts, p2, w_gu, b_gu, w_down, b_down, ln2_g, ln2_b, w_ple, w_ple_gate, b_ple_gate):
    dest, fill, block_e, n_valid, n_rows = _moe_layout(route, counts, h.shape[0])
    xs = _dispatch(dest, fill, h, n_rows)
    ys = _experts(xs, block_e, n_valid, w_gu[i].astype(BF16), b_gu[i][:, None, :], w_down[i].astype(BF16),
                  b_down[i][:, None, :])
    return _combine(dest, ys, gates, h, p2, w_ple_gate[i].astype(BF16), _row(b_ple_gate[i]), w_ple[i].astype(BF16),
                    _row(ln2_g[i]), _row(ln2_b[i]))


def kernel(x, p, even_w_in, even_gmlp_ln_g, even_gmlp_ln_b, even_w_s, even_b_s, even_w_o, odd_w_in, odd_w_o, ln1_g, ln1_b, w_router, b_router, w_gu, b_gu, w_down, b_down, ln2_g, ln2_b, w_ple, w_ple_gate, b_ple_gate):
    b, s, d = x.shape
    n = b * s
    x2 = x.reshape(n, d)
    for i in range(DEPTH):
        j = i // 2
        wr_hi, wr_lo, br = _router_weights(w_router[i], b_router[i])
        if i % 2 == 0:
            bs_full = jnp.repeat(even_b_s[j].T, A_WIDTH // A_GROUPS, axis=1)
            a_out, qkv = _even_in(x2, even_w_in[j].astype(BF16), _row(even_gmlp_ln_g[j]), _row(even_gmlp_ln_b[j]),
                                  even_w_s[j], bs_full)
            b_out = _stick_breaking(qkv.reshape(b, s, -1)).reshape(n, B_WIDTH)
            w_o = even_w_o[j].astype(BF16)
            parts, weights = [a_out, b_out], [w_o[:A_WIDTH], w_o[A_WIDTH:]]
        else:
            w_in = jnp.pad(odd_w_in[j], ((0, 0), (0, C_WIDTH + 2 * C_KV_WIDTH + IDX_HEADS * IDX_DIM + ODD_TAIL
                                                  - odd_w_in.shape[2]))).astype(BF16)
            q, kv, qidx, tail = _odd_in(x2, w_in)
            kidx = tail[:, :IDX_DIM].astype(BF16).reshape(b, s, IDX_DIM)
            kidx2 = jnp.concatenate([kidx, kidx], axis=-1)
            widx = tail[:, IDX_DIM:IDX_DIM + IDX_HEADS].reshape(b, s, IDX_HEADS)
            kg = kv[:, :C_KV_WIDTH].reshape(b, s, C_KV_GROUPS, HEAD_DIM).transpose(0, 2, 1, 3)
            vg = kv[:, C_KV_WIDTH:].reshape(b, s, C_KV_GROUPS, HEAD_DIM).transpose(0, 2, 1, 3)
            kdup = jnp.concatenate([kg, kg], axis=-1)
            vdup = jnp.concatenate([vg, vg], axis=-1)
            topk = min(IDX_TOPK_MAX, s // 4)
            o = _dsa(qidx.reshape(b, s, -1), widx, kidx2, q.reshape(b, s, -1), kdup, vdup, topk)
            parts, weights = [o.reshape(n, C_WIDTH)], [odd_w_o[j].astype(BF16)]
        h, route, gates, counts = _post_mixer(parts, weights, x2, _row(ln1_g[i]), _row(ln1_b[i]), wr_hi, wr_lo, br)
        x2 = _channel_mixer(i, h, route, gates, counts, p[i].reshape(n, PLE_DIM),
                            w_gu, b_gu, w_down, b_down, ln2_g, ln2_b, w_ple, w_ple_gate, b_ple_gate)
    return x2.reshape(b, s, d)
```

```python
import functools

import jax
import jax.numpy as jnp
from jax import lax
from jax.experimental import pallas as pl
from jax.experimental.pallas import tpu as pltpu

F32 = jnp.float32
BF16 = jnp.bfloat16
I32 = jnp.int32

D_MODEL = 1024
DEPTH = 2
CHUNK = 64
A_WIDTH = 512
A_GROUPS = 8
GMLP_CHUNK = 128
HEAD_DIM = 64
B_HEADS = 8
B_WIDTH = 512
C_HEADS = 16
C_KV_GROUPS = 4
C_HEADS_PER_GROUP = 4
C_WIDTH = 1024
C_KV_WIDTH = 256
IDX_HEADS = 8
IDX_DIM = 64
IDX_TOPK_MAX = 256
N_EXPERTS = 32
TOP_K = 4
D_EXPERT = 1024
SWIGLU_ALPHA = 1.702
SWIGLU_LIMIT = 7.0
MOE_BLOCK = 256
PLE_DIM = 256
LN_EPS = 1e-5
DEEPNORM_ALPHA = (2 * DEPTH) ** 0.25

LANES = 128
SUBLANES = 8
VMEM_LIMIT = 48 * 1024 * 1024

NEG_BIG = -1e30
INT_MIN = -(2 ** 31)


def _cparams(*sem):
    return pltpu.CompilerParams(dimension_semantics=sem, vmem_limit_bytes=VMEM_LIMIT)


def _dot(a, b):
    return jnp.dot(a, b, preferred_element_type=F32)


def _dot_nt(a, b):
    return lax.dot_general(a, b, (((1,), (1,)), ((), ())), preferred_element_type=F32)


def _layer_norm(y, g, b):
    mu = jnp.mean(y, axis=-1, keepdims=True)
    yc = y - mu
    var = jnp.mean(yc * yc, axis=-1, keepdims=True)
    return yc * lax.rsqrt(var + LN_EPS) * g + b


def _sigmoid(x):
    return 1.0 / (1.0 + jnp.exp(-x))


EVEN_TM = 256


def _even_in_kernel(x_ref, w_ref, lng_ref, lnb_ref, ws_ref, bs_ref, a_ref, qkv_ref):
    tm = x_ref.shape[0]
    xb = x_ref[...].astype(BF16)
    za = _dot(xb, w_ref[:, : 2 * A_WIDTH])
    ga = 0.5 * za * (1.0 + jnp.tanh(0.7978845608028654 * (za + 0.044715 * (za * za * za))))
    u = ga[:, :A_WIDTH]
    vn = _layer_norm(ga[:, A_WIDTH:], lng_ref[...], lnb_ref[...]).astype(BF16)

    r = lax.broadcasted_iota(I32, (GMLP_CHUNK, GMLP_CHUNK), 0)
    c = lax.broadcasted_iota(I32, (GMLP_CHUNK, GMLP_CHUNK), 1)
    tril = c <= r
    w_s = [jnp.where(tril, ws_ref[g], 0.0).astype(BF16) for g in range(A_GROUPS)]
    low_half = lax.broadcasted_iota(I32, (GMLP_CHUNK, LANES), 1) < A_WIDTH // A_GROUPS
    for ci in range(tm // GMLP_CHUNK):
        rows = slice(ci * GMLP_CHUNK, (ci + 1) * GMLP_CHUNK)
        for p in range(A_GROUPS // 2):
            cols = slice(p * LANES, (p + 1) * LANES)
            vb = vn[rows, cols]
            sv = jnp.where(low_half, _dot(w_s[2 * p], vb), _dot(w_s[2 * p + 1], vb)) + bs_ref[:, cols]
            a_ref[rows, cols] = (u[rows, cols] * sv).astype(BF16)

    nq = qkv_ref.shape[1]
    for j in range(0, nq, 512):
        qkv_ref[:, j:j + 512] = _dot(xb, w_ref[:, 2 * A_WIDTH + j: 2 * A_WIDTH + j + 512]).astype(BF16)


def _even_in(x2, w_in, ln_g, ln_b, w_s, bs_full):
    n, d = x2.shape
    nq = w_in.shape[1] - 2 * A_WIDTH
    return pl.pallas_call(
        _even_in_kernel,
        grid=(n // EVEN_TM,),
        in_specs=[
            pl.BlockSpec((EVEN_TM, d), lambda i: (i, 0)),
            pl.BlockSpec(w_in.shape, lambda i: (0, 0)),
            pl.BlockSpec((1, A_WIDTH), lambda i: (0, 0)),
            pl.BlockSpec((1, A_WIDTH), lambda i: (0, 0)),
            pl.BlockSpec(w_s.shape, lambda i: (0, 0, 0)),
            pl.BlockSpec(bs_full.shape, lambda i: (0, 0)),
        ],
        out_specs=[
            pl.BlockSpec((EVEN_TM, A_WIDTH), lambda i: (i, 0)),
            pl.BlockSpec((EVEN_TM, nq), lambda i: (i, 0)),
        ],
        out_shape=[
            jax.ShapeDtypeStruct((n, A_WIDTH), BF16),
            jax.ShapeDtypeStruct((n, nq), BF16),
        ],
        compiler_params=_cparams("parallel"),
        name="even_in",
    )(x2, w_in, ln_g, ln_b, w_s, bs_full)


SB_T = 128
SB_UNDERFLOW = -104.0


def _sb_kernel(q_ref, k_ref, v_ref, o_ref):
    t = q_ref.shape[0]
    npair = q_ref.shape[1] // LANES
    qi = pl.program_id(1)
    low = lax.broadcasted_iota(I32, (t, LANES), 1) < HEAD_DIM
    q2s = []
    for p in range(npair):
        q = q_ref[:, p * LANES:(p + 1) * LANES]
        zero = jnp.zeros_like(q)
        q2s.append(jnp.concatenate([jnp.where(low, q, zero), jnp.where(low, zero, q)], axis=0) * (HEAD_DIM ** -0.5))
    r = lax.broadcasted_iota(I32, (t, t), 0)
    c = lax.broadcasted_iota(I32, (t, t), 1)
    suffix = (r > c).astype(BF16)
    r2 = lax.broadcasted_iota(I32, (2 * t, t), 0)
    c2 = lax.broadcasted_iota(I32, (2 * t, t), 1)
    strict = c2 < jnp.where(r2 >= t, r2 - t, r2)

    def block(j, carries, accs, diag):
        start = pl.multiple_of(j * t, t)
        pairs = range(npair)
        cols = [slice(p * LANES, (p + 1) * LANES) for p in pairs]
        ss = [_dot_nt(q2s[p], k_ref[pl.ds(start, t), cols[p]]) for p in pairs]
        lns = []
        for p in pairs:
            log_not = -(jnp.maximum(ss[p], 0.0) + jnp.log(1.0 + jnp.exp(-jnp.abs(ss[p]))))
            if diag:
                log_not = jnp.where(strict, log_not, 0.0)
            lns.append(log_not)
        his = [ln.astype(BF16) for ln in lns]
        los = [(ln - hi.astype(F32)).astype(BF16) for ln, hi in zip(lns, his)]
        afters = [_dot(his[p], suffix) + _dot(los[p], suffix) + carries[p] for p in pairs]
        new_c, new_a = [], []
        for p in pairs:
            w = jnp.exp(ss[p] + lns[p] + afters[p])
            if diag:
                w = jnp.where(strict, w, 0.0)
            new_a.append(accs[p] + _dot(w.astype(BF16), v_ref[pl.ds(start, t), cols[p]]))
            new_c.append(carries[p] + jnp.sum(lns[p], axis=1, keepdims=True))
        return tuple(new_c), tuple(new_a)

    carries, accs = block(qi, (jnp.zeros((2 * t, 1), F32),) * npair, (jnp.zeros((2 * t, LANES), F32),) * npair, True)

    def cond(st):
        return jnp.logical_and(st[0] < qi, jnp.max(functools.reduce(jnp.maximum, st[1])) > SB_UNDERFLOW)

    def body(st):
        it, carries, accs = st
        carries, accs = block(qi - 1 - it, carries, accs, False)
        return it + 1, carries, accs

    _, carries, accs = lax.while_loop(cond, body, (jnp.int32(0), carries, accs))
    for p in range(npair):
        o_ref[:, p * LANES:(p + 1) * LANES] = jnp.where(low, accs[p][:t], accs[p][t:]).astype(BF16)


def _stick_breaking(qkv3):
    b, s, _ = qkv3.shape
    return pl.pallas_call(
        _sb_kernel,
        grid=(b, s // SB_T),
        in_specs=[
            pl.BlockSpec((None, SB_T, B_WIDTH), lambda bi, qi: (bi, qi, 0)),
            pl.BlockSpec((None, s, B_WIDTH), lambda bi, qi: (bi, 0, 1)),
            pl.BlockSpec((None, s, B_WIDTH), lambda bi, qi: (bi, 0, 2)),
        ],
        out_specs=pl.BlockSpec((None, SB_T, B_WIDTH), lambda bi, qi: (bi, qi, 0)),
        out_shape=jax.ShapeDtypeStruct((b, s, B_WIDTH), BF16),
        compiler_params=_cparams("parallel", "arbitrary"),
        name="stick_breaking",
    )(qkv3, qkv3, qkv3)


POST_TM = 256


def _post_mixer_kernel(*refs, n_in):
    ins = refs[:n_in]
    ws = refs[n_in:2 * n_in]
    x_ref, g_ref, b_ref, wrh_ref, wrl_ref, br_ref, h_ref, route_ref, gate_ref, counts_ref, tiles_ref, cnt_ref = refs[2 * n_in:]
    tm = x_ref.shape[0]

    @pl.when(pl.program_id(0) == 0)
    def _():
        cnt_ref[...] = jnp.zeros_like(cnt_ref)

    mix = _dot(ins[0][...], ws[0][...])
    for a, w in zip(ins[1:], ws[1:]):
        mix = mix + _dot(a[...], w[...])
    h = _layer_norm(DEEPNORM_ALPHA * x_ref[...] + mix, g_ref[...], b_ref[...])
    h_ref[...] = h

    hh = h.astype(BF16)
    hl = (h - hh.astype(F32)).astype(BF16)
    logits = _dot(hh, wrh_ref[...]) + _dot(hl, wrh_ref[...]) + _dot(hh, wrl_ref[...]) + br_ref[...]

    lane = lax.broadcasted_iota(I32, (tm, LANES), 1)
    vals, idxs = [], []
    for _ in range(TOP_K):
        m = jnp.max(logits, axis=1, keepdims=True)
        i = jnp.min(jnp.where(logits == m, lane, LANES), axis=1, keepdims=True)
        vals.append(m)
        idxs.append(i)
        logits = jnp.where(lane == i, NEG_BIG * 2.0, logits)
    es = [jnp.exp(v - vals[0]) for v in vals]
    inv = 1.0 / (es[0] + es[1] + es[2] + es[3])

    sel = jnp.zeros((tm, LANES), F32)
    for k in range(TOP_K):
        sel = jnp.where(lane == idxs[k], 1.0, sel)
    r = lax.broadcasted_iota(I32, (tm, tm), 0)
    c = lax.broadcasted_iota(I32, (tm, tm), 1)
    incl = _dot((c <= r).astype(BF16), sel.astype(BF16))
    seen = cnt_ref[...]
    tile_cnt = incl[tm - 1:tm, :]
    cnt_ref[...] = seen + tile_cnt
    counts_ref[...] = cnt_ref[...]
    er = lax.broadcasted_iota(I32, (LANES, LANES), 0)
    ec = lax.broadcasted_iota(I32, (LANES, LANES), 1)
    run_off = _dot(jnp.broadcast_to(tile_cnt, (SUBLANES, LANES)).astype(BF16), (er < ec).astype(BF16))[0:1]
    local = incl - sel + run_off
    row8 = lax.broadcasted_iota(I32, (SUBLANES, LANES), 0)
    tiles_ref[...] = jnp.where(row8 == 0, seen, jnp.where(row8 == 1, tile_cnt, 0.0))

    route = jnp.zeros((tm, LANES), I32)
    gate_out = jnp.zeros((tm, LANES), F32)
    for k in range(TOP_K):
        route = jnp.where(lane == k, idxs[k], route)
        slot = jnp.sum(jnp.where(lane == idxs[k], local, 0.0), axis=1, keepdims=True).astype(I32)
        route = jnp.where(lane == TOP_K + k, slot, route)
        gate_out = jnp.where(lane == k, es[k] * inv, gate_out)
    route_ref[...] = route
    gate_ref[...] = gate_out


def _post_mixer(parts, weights, x2, ln_g, ln_b, wr_hi, wr_lo, br):
    n, d = x2.shape
    n_in = len(parts)
    in_specs = [pl.BlockSpec((POST_TM, a.shape[1]), lambda i: (i, 0)) for a in parts]
    in_specs += [pl.BlockSpec(w.shape, lambda i: (0, 0)) for w in weights]
    in_specs += [
        pl.BlockSpec((POST_TM, d), lambda i: (i, 0)),
        pl.BlockSpec((1, d), lambda i: (0, 0)),
        pl.BlockSpec((1, d), lambda i: (0, 0)),
        pl.BlockSpec(wr_hi.shape, lambda i: (0, 0)),
        pl.BlockSpec(wr_lo.shape, lambda i: (0, 0)),
        pl.BlockSpec((1, LANES), lambda i: (0, 0)),
    ]
    return pl.pallas_call(
        functools.partial(_post_mixer_kernel, n_in=n_in),
        grid=(n // POST_TM,),
        in_specs=in_specs,
        out_specs=[
            pl.BlockSpec((POST_TM, d), lambda i: (i, 0)),
            pl.BlockSpec((POST_TM, LANES), lambda i: (i, 0)),
            pl.BlockSpec((POST_TM, LANES), lambda i: (i, 0)),
            pl.BlockSpec((1, LANES), lambda i: (0, 0)),
            pl.BlockSpec((None, SUBLANES, LANES), lambda i: (i, 0, 0)),
        ],
        out_shape=[
            jax.ShapeDtypeStruct((n, d), F32),
            jax.ShapeDtypeStruct((n, LANES), I32),
            jax.ShapeDtypeStruct((n, LANES), F32),
            jax.ShapeDtypeStruct((1, LANES), F32),
            jax.ShapeDtypeStruct((n // POST_TM, SUBLANES, LANES), F32),
        ],
        scratch_shapes=[pltpu.VMEM((1, LANES), F32)],
        compiler_params=_cparams("arbitrary"),
        name="post_mixer",
    )(*parts, *weights, x2, ln_g, ln_b, wr_hi, wr_lo, br)


DISP_TM = POST_TM
RUN_PIECES = tuple(2 ** i for i in range(DISP_TM.bit_length() - 1, -1, -1))


def _run_pieces(tab_ref, e, off, make_copy):
    start = tab_ref[0, 0, e]
    length = tab_ref[0, 0, N_EXPERTS + e]
    pos = jnp.int32(0)
    for size in RUN_PIECES:
        has = (length & size) != 0

        @pl.when(has)
        def _():
            make_copy(pl.ds(pl.multiple_of((start + pos) * SUBLANES, SUBLANES), size * SUBLANES),
                      pl.ds(pl.multiple_of((off + pos) * SUBLANES, SUBLANES), size * SUBLANES)).start()

        pos = pos + jnp.where(has, size, 0)
    return off + length


def _dispatch_kernel(tab_ref, fill_ref, route_ref, h_ref, xs_hbm, buf, zero_ref, sem):
    tm = h_ref.shape[0]
    n_slots = tm * TOP_K
    route = route_ref[...]
    slot_lane = lax.broadcasted_iota(I32, (tm, n_slots), 1)
    hit = slot_lane == route[:, TOP_K:TOP_K + 1]
    for k in range(1, TOP_K):
        hit = hit | (slot_lane == route[:, TOP_K + k:TOP_K + k + 1])
    onehot = jnp.where(hit, 1.0, 0.0).astype(BF16)
    rows = lax.dot_general(onehot, h_ref[...].astype(BF16), (((0,), (0,)), ((), ())),
                           preferred_element_type=F32)
    i = pl.program_id(0)
    cur = i % 2

    def wait_runs(half):
        pltpu.make_async_copy(buf.at[half], xs_hbm.at[pl.ds(0, n_slots * SUBLANES), :], sem.at[half]).wait()

    @pl.when(i >= 2)
    def _():
        wait_runs(cur)

    for j in range(SUBLANES):
        buf[cur, pl.ds(j, n_slots, stride=SUBLANES), :] = rows[:, j * LANES:(j + 1) * LANES]

    def copy(dst_rows, src_rows):
        return pltpu.make_async_copy(buf.at[cur, src_rows, :], xs_hbm.at[dst_rows, :], sem.at[cur])

    lax.fori_loop(0, N_EXPERTS, lambda e, off: _run_pieces(tab_ref, e, off, copy), jnp.int32(0))

    @pl.when(i == pl.num_programs(0) - 1)
    def _():
        @pl.when(i >= 1)
        def _():
            wait_runs(1 - cur)

        wait_runs(cur)
        zero_ref[...] = jnp.zeros_like(zero_ref)
        fill_sem = sem.at[2]

        def row_copy(r):
            return pltpu.make_async_copy(zero_ref.at[pl.ds(0, SUBLANES), :],
                                         xs_hbm.at[pl.ds(pl.multiple_of(r * SUBLANES, SUBLANES), SUBLANES), :], fill_sem)

        def per_expert(e, c):
            first = fill_ref[e]
            n_fill = fill_ref[N_EXPERTS + e]

            def fill(j, c2):
                row_copy(first + j).start()
                return c2

            lax.fori_loop(0, n_fill, fill, 0)

            def fill_wait(j, c2):
                row_copy(first + j).wait()
                return c2

            lax.fori_loop(0, n_fill, fill_wait, 0)
            return c

        lax.fori_loop(0, N_EXPERTS, per_expert, 0)

        block_rows = MOE_BLOCK * SUBLANES

        def tail_copy(blk):
            return pltpu.make_async_copy(
                zero_ref, xs_hbm.at[pl.ds(pl.multiple_of(blk * block_rows, block_rows), block_rows), :], fill_sem)

        n_valid = fill_ref[2 * N_EXPERTS]
        n_blocks = xs_hbm.shape[0] // block_rows

        def tail(blk, c):
            tail_copy(blk).start()
            return c

        lax.fori_loop(n_valid, n_blocks, tail, 0)

        def tail_wait(blk, c):
            tail_copy(blk).wait()
            return c

        lax.fori_loop(n_valid, n_blocks, tail_wait, 0)


def _dispatch(tab, fill, route, h, n_rows):
    n, d = h.shape
    nt = n // DISP_TM
    return pl.pallas_call(
        _dispatch_kernel,
        grid=(nt,),
        in_specs=[
            pl.BlockSpec((1, 1, 2 * N_EXPERTS), lambda i: (i, 0, 0), memory_space=pltpu.SMEM),
            pl.BlockSpec(memory_space=pltpu.SMEM),
            pl.BlockSpec((DISP_TM, LANES), lambda i: (i, 0)),
            pl.BlockSpec((DISP_TM, d), lambda i: (i, 0)),
        ],
        out_specs=pl.BlockSpec(memory_space=pl.ANY),
        out_shape=jax.ShapeDtypeStruct((n_rows * SUBLANES, LANES), F32),
        scratch_shapes=[pltpu.VMEM((2, DISP_TM * TOP_K * SUBLANES, LANES), F32),
                        pltpu.VMEM((MOE_BLOCK * SUBLANES, LANES), F32), pltpu.SemaphoreType.DMA((3,))],
        compiler_params=_cparams("arbitrary"),
        name="moe_dispatch",
    )(tab, fill, route, h)


def _expert_kernel(be_ref, nv_ref, x_ref, wgu_ref, bgu_ref, wd_ref, bd_ref, y_ref):
    del be_ref

    @pl.when(pl.program_id(0) < nv_ref[0])
    def _():
        xb = jnp.concatenate([x_ref[pl.ds(j, MOE_BLOCK, stride=SUBLANES), :] for j in range(SUBLANES)],
                             axis=1).astype(BF16)
        gu = _dot(xb, wgu_ref[...]) + bgu_ref[...]
        gate = jnp.minimum(gu[:, :D_EXPERT], SWIGLU_LIMIT)
        up = jnp.clip(gu[:, D_EXPERT:], -SWIGLU_LIMIT, SWIGLU_LIMIT)
        act = gate * _sigmoid(SWIGLU_ALPHA * gate) * (up + 1.0)
        y = _dot(act.astype(BF16), wd_ref[...]) + bd_ref[...]
        for j in range(SUBLANES):
            y_ref[pl.ds(j, MOE_BLOCK, stride=SUBLANES), :] = y[:, j * LANES:(j + 1) * LANES]

    @pl.when(pl.program_id(0) >= nv_ref[0])
    def _():
        y_ref[...] = jnp.zeros_like(y_ref)


def _experts(xs, block_e, n_valid, w_gu, b_gu, w_down, b_down):
    d = D_MODEL
    n_rows = xs.shape[0] // SUBLANES
    nb = n_rows // MOE_BLOCK

    def rows(i, be, nv):
        return (i, 0)

    return pl.pallas_call(
        _expert_kernel,
        grid_spec=pltpu.PrefetchScalarGridSpec(
            num_scalar_prefetch=2,
            grid=(nb,),
            in_specs=[
                pl.BlockSpec((MOE_BLOCK * SUBLANES, LANES), rows),
                pl.BlockSpec((None, d, 2 * D_EXPERT), lambda i, be, nv: (be[i], 0, 0)),
                pl.BlockSpec((None, 1, 2 * D_EXPERT), lambda i, be, nv: (be[i], 0, 0)),
                pl.BlockSpec((None, D_EXPERT, d), lambda i, be, nv: (be[i], 0, 0)),
                pl.BlockSpec((None, 1, d), lambda i, be, nv: (be[i], 0, 0)),
            ],
            out_specs=pl.BlockSpec((MOE_BLOCK * SUBLANES, LANES), rows),
        ),
        out_shape=jax.ShapeDtypeStruct((n_rows * SUBLANES, LANES), F32),
        compiler_params=_cparams("arbitrary"),
        name="moe_experts",
    )(block_e, n_valid, xs, w_gu, b_gu, w_down, b_down)


COMB_TM = POST_TM


def _combine_kernel(tab_ref, next_tab_ref, ys_hbm, route_ref, gate_ref, h_ref, p_ref, wpg_ref, bpg_ref, wple_ref,
                    g_ref, b_ref, o_ref, buf, sem):
    tm = h_ref.shape[0]
    n_slots = tm * TOP_K
    i = pl.program_id(0)
    cur = i % 2

    def fetch(tab, half):
        def copy(src_rows, dst_rows):
            return pltpu.make_async_copy(ys_hbm.at[src_rows, :], buf.at[half, dst_rows, :], sem.at[half])
        lax.fori_loop(0, N_EXPERTS, lambda e, off: _run_pieces(tab, e, off, copy), jnp.int32(0))

    @pl.when(i == 0)
    def _():
        fetch(tab_ref, cur)

    @pl.when(i + 1 < pl.num_programs(0))
    def _():
        fetch(next_tab_ref, 1 - cur)

    pltpu.make_async_copy(ys_hbm.at[pl.ds(0, n_slots * SUBLANES), :], buf.at[cur], sem.at[cur]).wait()

    y = jnp.concatenate([buf[cur, pl.ds(j, n_slots, stride=SUBLANES), :] for j in range(SUBLANES)], axis=1)
    yh = y.astype(BF16)
    route = route_ref[...]
    gates = gate_ref[...]
    slot_lane = lax.broadcasted_iota(I32, (tm, n_slots), 1)
    gmat = jnp.zeros((tm, n_slots), F32)
    for k in range(TOP_K):
        gmat = jnp.where(slot_lane == route[:, TOP_K + k:TOP_K + k + 1], gates[:, k:k + 1], gmat)
    gh = gmat.astype(BF16)
    gl = (gmat - gh.astype(F32)).astype(BF16)
    moe = _dot(gh, yh) + _dot(gl, yh)
    mid = DEEPNORM_ALPHA * h_ref[...] + moe
    gate = _sigmoid(_dot(mid.astype(BF16), wpg_ref[...]) + bpg_ref[...])
    ple = _dot(p_ref[...].astype(BF16), wple_ref[...])
    o_ref[...] = _layer_norm(mid + gate * ple, g_ref[...], b_ref[...])


def _combine(tab, ys, route, gates, h, p2, w_pg, b_pg, w_ple, ln_g, ln_b):
    n, d = h.shape
    nt = n // COMB_TM
    return pl.pallas_call(
        _combine_kernel,
        grid=(nt,),
        in_specs=[
            pl.BlockSpec((1, 1, 2 * N_EXPERTS), lambda i: (i, 0, 0), memory_space=pltpu.SMEM),
            pl.BlockSpec((1, 1, 2 * N_EXPERTS), lambda i: (jnp.minimum(i + 1, nt - 1), 0, 0), memory_space=pltpu.SMEM),
            pl.BlockSpec(memory_space=pl.ANY),
            pl.BlockSpec((COMB_TM, LANES), lambda i: (i, 0)),
            pl.BlockSpec((COMB_TM, LANES), lambda i: (i, 0)),
            pl.BlockSpec((COMB_TM, d), lambda i: (i, 0)),
            pl.BlockSpec((COMB_TM, PLE_DIM), lambda i: (i, 0)),
            pl.BlockSpec(w_pg.shape, lambda i: (0, 0)),
            pl.BlockSpec((1, d), lambda i: (0, 0)),
            pl.BlockSpec(w_ple.shape, lambda i: (0, 0)),
            pl.BlockSpec((1, d), lambda i: (0, 0)),
            pl.BlockSpec((1, d), lambda i: (0, 0)),
        ],
        out_specs=pl.BlockSpec((COMB_TM, d), lambda i: (i, 0)),
        out_shape=jax.ShapeDtypeStruct((n, d), F32),
        scratch_shapes=[pltpu.VMEM((2, COMB_TM * TOP_K * SUBLANES, LANES), F32), pltpu.SemaphoreType.DMA((2,))],
        compiler_params=_cparams("arbitrary"),
        name="moe_combine",
    )(tab, tab, ys, route, gates, h, p2, w_pg, b_pg, w_ple, ln_g, ln_b)


def _moe_layout(counts, tiles, n_tok):
    n_blocks = n_tok * TOP_K // MOE_BLOCK + N_EXPERTS
    counts = counts[0, :N_EXPERTS].astype(I32)
    padded = (counts + MOE_BLOCK - 1) // MOE_BLOCK * MOE_BLOCK
    pad_ends = jnp.cumsum(padded)
    pad_starts = pad_ends - padded
    n_valid = (pad_ends[-1:] // MOE_BLOCK).astype(I32)
    run_start = pad_starts[None, :] + tiles[:, 0, :N_EXPERTS].astype(I32)
    tab = jnp.concatenate([run_start, tiles[:, 1, :N_EXPERTS].astype(I32)], axis=1)[:, None, :]
    fill = jnp.concatenate([pad_starts + counts, padded - counts, n_valid]).astype(I32)
    block_start = jnp.arange(n_blocks, dtype=I32) * MOE_BLOCK
    block_e = jnp.minimum(jnp.sum((pad_ends[None, :] <= block_start[:, None]).astype(I32), axis=1), N_EXPERTS - 1)
    return tab, fill, block_e.astype(I32), n_valid, n_blocks * MOE_BLOCK


ODD_TM = 512
ODD_TAIL = 128
LOG2_E = 1.4426950408889634
DSA_Q_SCALE = HEAD_DIM ** -0.5 * LOG2_E


def _odd_in_kernel(x_ref, w_ref, q_ref, kv_ref, qi_ref, tail_ref):
    xb = x_ref[...].astype(BF16)
    off = 0
    for ref, scale in ((q_ref, DSA_Q_SCALE), (kv_ref, None), (qi_ref, None)):
        n = ref.shape[1]
        for j in range(0, n, 512):
            z = _dot(xb, w_ref[:, off + j: off + j + 512])
            ref[:, j:j + 512] = (z if scale is None else z * scale).astype(ref.dtype)
        off += n
    tail_ref[...] = _dot(xb, w_ref[:, off:off + ODD_TAIL])


def _odd_in(x2, w_in):
    n, d = x2.shape
    widths = (C_WIDTH, 2 * C_KV_WIDTH, IDX_HEADS * IDX_DIM)
    return pl.pallas_call(
        _odd_in_kernel,
        grid=(n // ODD_TM,),
        in_specs=[
            pl.BlockSpec((ODD_TM, d), lambda i: (i, 0)),
            pl.BlockSpec(w_in.shape, lambda i: (0, 0)),
        ],
        out_specs=[pl.BlockSpec((ODD_TM, w), lambda i: (i, 0)) for w in widths]
        + [pl.BlockSpec((ODD_TM, ODD_TAIL), lambda i: (i, 0))],
        out_shape=[jax.ShapeDtypeStruct((n, w), BF16) for w in widths]
        + [jax.ShapeDtypeStruct((n, ODD_TAIL), F32)],
        compiler_params=_cparams("parallel"),
        name="odd_in",
    )(x2, w_in)


DSA_TQ = 128
DSA_KC = 512
DSA_V_ROWS = HEAD_DIM + 16


def _dsa_kernel(qiT_ref, wT_ref, kidx_ref, qT_ref, k_ref, vT_ref, o_ref, key_ref, dm_ref, *, topk):
    tq, kc = DSA_TQ, DSA_KC
    t0 = pl.program_id(1) * tq
    nch = (t0 + tq + kc - 1) // kc
    k_pos = lax.broadcasted_iota(I32, (kc, tq), 0)
    t_pos = t0 + lax.broadcasted_iota(I32, (kc, tq), 1)
    vis_end = (t_pos // CHUNK + 1) * CHUNK

    def key_sum(x):
        z = jnp.sum(x.reshape(4, kc // 32, 8, x.shape[1]), axis=1)
        return (z[0] + z[1]) + (z[2] + z[3])

    qi = qiT_ref[...]
    wv = wT_ref[...] * (IDX_HEADS ** -0.5 * IDX_DIM ** -0.5)

    def score_chunk(c, carry):
        kk = kidx_ref[pl.ds(pl.multiple_of(c * kc, kc), kc), :]
        s_all = _dot(kk, qi)
        acc = jnp.zeros((kc, tq), F32)
        for h in range(IDX_HEADS):
            acc = acc + wv[h:h + 1, :] * jnp.maximum(s_all[:, h * tq:(h + 1) * tq], 0.0)
        acc = jnp.where(acc == 0.0, 0.0, acc)
        bits = pltpu.bitcast(acc, I32)
        key = jnp.where(bits < 0, bits ^ 0x7FFFFFFF, bits)
        key_ref[c] = jnp.where(c * kc + k_pos < vis_end, key, INT_MIN)
        return carry

    lax.fori_loop(0, nch, score_chunk, 0)

    def count(pred_fn):
        def body(c, acc):
            return acc + key_sum(jnp.where(pred_fn(key_ref[c]), 1.0, 0.0))
        return jnp.sum(lax.fori_loop(0, nch, body, jnp.zeros((8, tq), F32)), axis=0, keepdims=True)

    def bit_step(it, t_u):
        cand_u = t_u | jnp.left_shift(jnp.ones((1, tq), I32), 31 - it)
        cand_s = cand_u ^ INT_MIN
        cnt = count(lambda kk: kk >= cand_s)
        return jnp.where(cnt >= topk, cand_u, t_u)

    t_u = lax.fori_loop(0, 32, bit_step, jnp.zeros((1, tq), I32))
    thr = t_u ^ INT_MIN
    need = topk - count(lambda kk: kk > thr)

    rr = lax.broadcasted_iota(I32, (kc, kc), 0)
    cc = lax.broadcasted_iota(I32, (kc, kc), 1)
    prefix = (cc <= rr).astype(BF16)

    def select_chunk(c, seen):
        kk = key_ref[c]
        eq = kk == thr
        eqf = jnp.where(eq, 1.0, 0.0)
        rank = _dot(prefix, eqf.astype(BF16)) + seen
        sel = ((kk > thr) | (eq & (rank <= need))) & (kk != INT_MIN)
        dist = jnp.abs(t_pos - (c * kc + k_pos)).astype(F32)
        dm_ref[c] = jnp.where(sel, -dist, NEG_BIG)
        return seen + jnp.sum(key_sum(eqf), axis=0, keepdims=True)

    lax.fori_loop(0, nch, select_chunk, jnp.zeros((1, tq), F32))

    hpg = C_HEADS_PER_GROUP
    groups = range(C_KV_GROUPS)
    slopes = [[LOG2_E * 2.0 ** (-8.0 * (g * hpg + r + 1) / C_HEADS) for r in range(hpg)] for g in groups]

    def attn_chunk(c, state):
        start = pl.multiple_of(c * kc, kc)
        dm = dm_ref[c]
        scores = [_dot(k_ref[g, pl.ds(start, kc), :], qT_ref[g]) for g in groups]
        new_state = []
        for g in groups:
            m, acc = state[g]
            s = scores[g]
            lg = jnp.concatenate([s[:, r * tq:(r + 1) * tq] + slopes[g][r] * dm for r in range(hpg)], axis=1)
            m_new = jnp.maximum(m, jnp.max(lg, axis=0, keepdims=True))
            a = jnp.exp2(m - m_new)
            p = jnp.exp2(lg - m_new)
            acc = acc * a + _dot(vT_ref[g, c], p.astype(BF16))
            new_state.append((m_new, acc))
        return tuple(new_state)

    init = (jnp.full((1, hpg * tq), NEG_BIG * 4.0, F32), jnp.zeros((DSA_V_ROWS, hpg * tq), F32))
    state = lax.fori_loop(0, nch, attn_chunk, (init,) * C_KV_GROUPS)
    for g in groups:
        acc = state[g][1]
        o = acc[:HEAD_DIM] / acc[HEAD_DIM:HEAD_DIM + 1]
        for r in range(hpg):
            hd = g * hpg + r
            o_ref[hd * HEAD_DIM:(hd + 1) * HEAD_DIM, :] = o[:, r * tq:(r + 1) * tq].astype(BF16)


def _dsa(q, kv, qidx, tail, b, s):
    tq, kc = DSA_TQ, DSA_KC
    nt, nc = s // tq, s // kc
    hpg = C_HEADS_PER_GROUP
    topk = min(IDX_TOPK_MAX, s // 4)
    qT = q.reshape(b, nt, tq, C_KV_GROUPS, hpg, HEAD_DIM).transpose(0, 1, 3, 5, 4, 2).reshape(
        b, nt, C_KV_GROUPS, HEAD_DIM, hpg * tq)
    qiT = qidx.reshape(b, nt, tq, IDX_HEADS, IDX_DIM).transpose(0, 1, 4, 3, 2).reshape(b, nt, IDX_DIM, IDX_HEADS * tq)
    wT = tail[:, IDX_DIM:IDX_DIM + IDX_HEADS].reshape(b, s, IDX_HEADS).transpose(0, 2, 1)
    kidx = tail[:, :IDX_DIM].astype(BF16).reshape(b, s, IDX_DIM)
    kg = kv[:, :C_KV_WIDTH].reshape(b, s, C_KV_GROUPS, HEAD_DIM).transpose(0, 2, 1, 3)
    vT = kv[:, C_KV_WIDTH:].reshape(b, nc, kc, C_KV_GROUPS, HEAD_DIM).transpose(0, 3, 1, 4, 2)
    vT = jnp.concatenate([vT, jnp.ones(vT.shape[:3] + (DSA_V_ROWS - HEAD_DIM, kc), BF16)], axis=3)
    oT = pl.pallas_call(
        functools.partial(_dsa_kernel, topk=topk),
        grid=(b, nt),
        in_specs=[
            pl.BlockSpec((None, None, IDX_DIM, IDX_HEADS * tq), lambda bi, qi: (bi, qi, 0, 0)),
            pl.BlockSpec((None, IDX_HEADS, tq), lambda bi, qi: (bi, 0, qi)),
            pl.BlockSpec((None, s, IDX_DIM), lambda bi, qi: (bi, 0, 0)),
            pl.BlockSpec((None, None, C_KV_GROUPS, HEAD_DIM, hpg * tq), lambda bi, qi: (bi, qi, 0, 0, 0)),
            pl.BlockSpec((None, C_KV_GROUPS, s, HEAD_DIM), lambda bi, qi: (bi, 0, 0, 0)),
            pl.BlockSpec((None, C_KV_GROUPS, nc, DSA_V_ROWS, kc), lambda bi, qi: (bi, 0, 0, 0, 0)),
        ],
        out_specs=pl.BlockSpec((None, None, C_WIDTH, tq), lambda bi, qi: (bi, qi, 0, 0)),
        out_shape=jax.ShapeDtypeStruct((b, nt, C_WIDTH, tq), BF16),
        scratch_shapes=[
            pltpu.VMEM((nc, kc, tq), I32),
            pltpu.VMEM((nc, kc, tq), F32),
        ],
        compiler_params=_cparams("parallel", "arbitrary"),
        name="dsa",
    )(qiT, wT, kidx, qT, kg, vT)
    return oT.transpose(0, 1, 3, 2).reshape(b * s, C_WIDTH)


def _row(v):
    return v.reshape(1, -1)


def _router_weights(w_router, b_router):
    pad = LANES - N_EXPERTS
    w = jnp.pad(w_router, ((0, 0), (0, pad)))
    hi = w.astype(BF16)
    lo = (w - hi.astype(F32)).astype(BF16)
    br = jnp.pad(b_router, (0, pad), constant_values=NEG_BIG).reshape(1, LANES)
    return hi, lo, br


def _channel_mixer(i, h, route, gates, counts, tiles, p2, w_gu, b_gu, w_down, b_down, ln2_g, ln2_b, w_ple, w_ple_gate, b_ple_gate):
    tab, fill, block_e, n_valid, n_rows = _moe_layout(counts, tiles, h.shape[0])
    xs = _dispatch(tab, fill, route, h, n_rows)
    ys = _experts(xs, block_e, n_valid, w_gu[i].astype(BF16), b_gu[i][:, None, :], w_down[i].astype(BF16),
                  b_down[i][:, None, :])
    return _combine(tab, ys, route, gates, h, p2, w_ple_gate[i].astype(BF16), _row(b_ple_gate[i]), w_ple[i].astype(BF16),
                    _row(ln2_g[i]), _row(ln2_b[i]))


def kernel(x, p, even_w_in, even_gmlp_ln_g, even_gmlp_ln_b, even_w_s, even_b_s, even_w_o, odd_w_in, odd_w_o, ln1_g, ln1_b, w_router, b_router, w_gu, b_gu, w_down, b_down, ln2_g, ln2_b, w_ple, w_ple_gate, b_ple_gate):
    b, s, d = x.shape
    n = b * s
    x2 = x.reshape(n, d)
    for i in range(DEPTH):
        j = i // 2
        wr_hi, wr_lo, br = _router_weights(w_router[i], b_router[i])
        if i % 2 == 0:
            bs_full = jnp.repeat(even_b_s[j].T, A_WIDTH // A_GROUPS, axis=1)
            a_out, qkv = _even_in(x2, even_w_in[j].astype(BF16), _row(even_gmlp_ln_g[j]), _row(even_gmlp_ln_b[j]),
                                  even_w_s[j], bs_full)
            b_out = _stick_breaking(qkv.reshape(b, s, -1)).reshape(n, B_WIDTH)
            w_o = even_w_o[j].astype(BF16)
            parts, weights = [a_out, b_out], [w_o[:A_WIDTH], w_o[A_WIDTH:]]
        else:
            w_in = jnp.pad(odd_w_in[j], ((0, 0), (0, C_WIDTH + 2 * C_KV_WIDTH + IDX_HEADS * IDX_DIM + ODD_TAIL
                                                  - odd_w_in.shape[2]))).astype(BF16)
            q, kv, qidx, tail = _odd_in(x2, w_in)
            o = _dsa(q, kv, qidx, tail, b, s)
            parts, weights = [o], [odd_w_o[j].astype(BF16)]
        h, route, gates, counts, tiles = _post_mixer(parts, weights, x2, _row(ln1_g[i]), _row(ln1_b[i]), wr_hi, wr_lo, br)
        x2 = _channel_mixer(i, h, route, gates, counts, tiles, p[i].reshape(n, PLE_DIM),
                            w_gu, b_gu, w_down, b_down, ln2_g, ln2_b, w_ple, w_ple_gate, b_ple_gate)
    return x2.reshape(b, s, d)
```
